```python
import jax, jax.numpy as jnp
from jax import lax
import numpy as np

D_MODEL = 1024
BATCH = 16
SEQ = 4096
DEPTH = 1
DEC_BATCH = 32
DEC_SEQ = 16
PAST_LEN = 1024

CHUNK = 64
N_HEADS = 16
HEAD_DIM = D_MODEL // N_HEADS
D_ATTN = N_HEADS * HEAD_DIM
D_CONV = D_MODEL
CONV_WIDTH = 3
D_FF = 4 * D_MODEL
Q_BLOCK = 128
RMS_EPS = 1e-6
N_PROJ = 3 * D_CONV + 3 * D_ATTN + 2 * D_MODEL
_SPLIT = [D_CONV, 2 * D_CONV, 3 * D_CONV,
          3 * D_CONV + D_ATTN, 3 * D_CONV + 2 * D_ATTN, 3 * D_CONV + 3 * D_ATTN,
          3 * D_CONV + 3 * D_ATTN + D_MODEL]

kernel_name = 'gated_conv_stickbreak_hybrid_step'


def _rmsnorm(x, g):
    x32 = x.astype(jnp.float32)
    y = x32 * lax.rsqrt(jnp.mean(x32 * x32, axis=-1, keepdims=True) + RMS_EPS)
    return (y * g.astype(jnp.float32)).astype(x.dtype)


def _project(xn, w_in):
    z = jnp.einsum('btd,dn->btn', xn, w_in)
    b_g, c_g, h, q, k, v, g_conv, g_attn = jnp.split(z, _SPLIT, axis=-1)
    bsz, t = xn.shape[0], xn.shape[1]
    q = q.reshape(bsz, t, N_HEADS, HEAD_DIM)
    k = k.reshape(bsz, t, N_HEADS, HEAD_DIM)
    v = v.reshape(bsz, t, N_HEADS, HEAD_DIM)
    return b_g, c_g, h, q, k, v, g_conv, g_attn


def _causal_conv(u_ext, conv_w, t):
    out = u_ext[:, 0:t] * conv_w[0]
    for i in range(1, CONV_WIDTH):
        out = out + u_ext[:, i:i + t] * conv_w[i]
    return out


def _stick_breaking(q, k, v, q_pos, k_pos):
    z = jnp.einsum('bqhd,bkhd->bhqk', q.astype(jnp.float32), k.astype(jnp.float32)) * (HEAD_DIM ** -0.5)
    mask = (k_pos[None, :] < q_pos[:, None])[None, None]
    log_rest = jnp.where(mask, jax.nn.log_sigmoid(-z), 0.0)
    suffix = lax.cumsum(log_rest, axis=3, reverse=True) - log_rest
    a = jnp.where(mask, jnp.exp(jax.nn.log_sigmoid(z) + suffix), 0.0)
    o = jnp.einsum('bhqk,bkhd->bqhd', a, v.astype(jnp.float32))
    return o.astype(v.dtype)


def _prompt_attention(q, k, v):
    bsz, s = q.shape[0], q.shape[1]
    nb = s // Q_BLOCK
    qb = jnp.moveaxis(q.reshape(bsz, nb, Q_BLOCK, N_HEADS, HEAD_DIM), 1, 0)
    pos = jnp.arange(s, dtype=jnp.int32)
    pb = pos.reshape(nb, Q_BLOCK)
    out = lax.map(lambda a: _stick_breaking(a[0], k, v, a[1], pos), (qb, pb))
    return jnp.moveaxis(out, 0, 1).reshape(bsz, s, N_HEADS, HEAD_DIM)


def _merge_ffn(x, conv_out, attn_out, g_conv, g_attn, w_out, g_ffn, w_up, w_down):
    bsz, t = x.shape[0], x.shape[1]
    mixed = jax.nn.sigmoid(g_conv) * conv_out + jax.nn.sigmoid(g_attn) * attn_out.reshape(bsz, t, D_ATTN)
    h = x + jnp.einsum('btc,cd->btd', mixed, w_out)
    up = jnp.einsum('btd,df->btf', _rmsnorm(h, g_ffn), w_up)
    return h + jnp.einsum('btf,fd->btd', jnp.square(jax.nn.relu(up)), w_down)


def setup_inputs(seed: int = 0) -> dict:
    key = jax.random.key(seed)
    ks = jax.random.split(key, 14)
    f32 = jnp.float32
    return {
        'x_prompt': jax.random.normal(ks[0], (BATCH, SEQ, D_MODEL), f32),
        'x_sample': jax.random.normal(ks[1], (DEC_BATCH, DEC_SEQ, D_MODEL), f32),
        'cache_conv': jax.random.normal(ks[2], (DEPTH, DEC_BATCH, CONV_WIDTH - 1, D_CONV), f32),
        'cache_k': jax.random.normal(ks[3], (DEPTH, DEC_BATCH, PAST_LEN, N_HEADS, HEAD_DIM), f32),
        'cache_v': jax.random.normal(ks[4], (DEPTH, DEC_BATCH, PAST_LEN, N_HEADS, HEAD_DIM), f32),
        'g_mix': 1.0 + 0.02 * jax.random.normal(ks[5], (DEPTH, D_MODEL), f32),
        'w_in': jax.random.normal(ks[6], (DEPTH, D_MODEL, N_PROJ), f32) * D_MODEL ** -0.5,
        'conv_w': jax.random.normal(ks[7], (DEPTH, CONV_WIDTH, D_CONV), f32) * CONV_WIDTH ** -0.5,
        'w_out': jax.random.normal(ks[8], (DEPTH, D_MODEL, D_MODEL), f32) * D_MODEL ** -0.5,
        'g_ffn': 1.0 + 0.02 * jax.random.normal(ks[9], (DEPTH, D_MODEL), f32),
        'w_up': jax.random.normal(ks[10], (DEPTH, D_MODEL, D_FF), f32) * D_MODEL ** -0.5,
        'w_down': jax.random.normal(ks[11], (DEPTH, D_FF, D_MODEL), f32) * D_FF ** -0.5,
        'g_final': 1.0 + 0.02 * jax.random.normal(ks[12], (D_MODEL,), f32),
    }


def reference(x_prompt, x_sample, cache_conv, cache_k, cache_v, g_mix, w_in, conv_w, w_out,
              g_ffn, w_up, w_down, g_final):
    xp, xs = x_prompt, x_sample
    seq = xp.shape[1]
    dec_seq = xs.shape[1]
    past = cache_k.shape[2]
    conv_p, k_p, v_p, conv_s, k_s, v_s = [], [], [], [], [], []
    for l in range(DEPTH):
        b_g, c_g, hc, q, k, v, g_c, g_a = _project(_rmsnorm(xp, g_mix[l]), w_in[l])
        u_ext = jnp.concatenate([jnp.zeros((xp.shape[0], CONV_WIDTH - 1, D_CONV), xp.dtype), c_g * hc], axis=1)
        conv_out = b_g * _causal_conv(u_ext, conv_w[l], seq)
        attn_out = _prompt_attention(q, k, v)
        conv_p.append(u_ext[:, -(CONV_WIDTH - 1):])
        k_p.append(k)
        v_p.append(v)
        xp = _merge_ffn(xp, conv_out, attn_out, g_c, g_a, w_out[l], g_ffn[l], w_up[l], w_down[l])
        b_g, c_g, hc, q, k, v, g_c, g_a = _project(_rmsnorm(xs, g_mix[l]), w_in[l])
        u_ext = jnp.concatenate([cache_conv[l].astype(xs.dtype), c_g * hc], axis=1)
        conv_out = b_g * _causal_conv(u_ext, conv_w[l], dec_seq)
        k_all = jnp.concatenate([cache_k[l].astype(k.dtype), k], axis=1)
        v_all = jnp.concatenate([cache_v[l].astype(v.dtype), v], axis=1)
        q_pos = past + jnp.arange(dec_seq, dtype=jnp.int32)
        k_pos = jnp.arange(past + dec_seq, dtype=jnp.int32)
        attn_out = _stick_breaking(q, k_all, v_all, q_pos, k_pos)
        conv_s.append(u_ext[:, -(CONV_WIDTH - 1):])
        k_s.append(k)
        v_s.append(v)
        xs = _merge_ffn(xs, conv_out, attn_out, g_c, g_a, w_out[l], g_ffn[l], w_up[l], w_down[l])
    y_prompt = _rmsnorm(xp, g_final)
    y_sample = _rmsnorm(xs, g_final)
    return (y_prompt, y_sample, jnp.stack(conv_p), jnp.stack(k_p), jnp.stack(v_p),
            jnp.stack(conv_s), jnp.stack(k_s), jnp.stack(v_s))
```

```python
import functools

import jax
import jax.numpy as jnp
from jax import lax
from jax.experimental import pallas as pl
from jax.experimental.pallas import tpu as pltpu

RMS_EPS = 1e-6
CONV_WIDTH = 3
N_PROJ_GROUPS = 8
LANES = 128
KEY_BLOCK = 128
HEADS_PER_SLAB = 2
CONV_TAIL = 8
VMEM_LIMIT_BYTES = 60 * 1024 * 1024
CARRY_DONE = 104.0

_F32 = jnp.float32
_BF16 = jnp.bfloat16


def _rms_scale(x32):
    return lax.rsqrt(jnp.mean(x32 * x32, axis=-1, keepdims=True) + RMS_EPS)


def _sigmoid(z):
    return 1.0 / (1.0 + jnp.exp(-z))


def _proj_kernel(*refs, d, tm, seg, scale):
    carried = seg == tm
    if carried:
        (x_ref, g_ref, w_ref, cw_ref,
         q_ref, k_ref, v_ref, kb_ref, vb_ref, mc_ref, ga_ref, tail_ref,
         xn_ref, u_ref, conv_ref) = refs
    else:
        (x_ref, g_ref, w_ref, cw_ref, p1_ref, p2_ref,
         q_ref, k_ref, v_ref, kb_ref, vb_ref, mc_ref, ga_ref, tail_ref,
         xn_ref, u_ref, conv_ref) = refs

    x32 = x_ref[...]
    xn_ref[...] = (x32 * _rms_scale(x32) * g_ref[...]).astype(_BF16)

    def proj(group):
        return jnp.dot(xn_ref[...], w_ref[:, group * d:(group + 1) * d], preferred_element_type=_F32)

    u = proj(1) * proj(2)
    cw = cw_ref[...]
    if carried:
        j = pl.program_id(1)

        @pl.when(j == 0)
        def _():
            u_ref[0:CONV_TAIL, :] = jnp.zeros((CONV_TAIL, d), _F32)

        u_ref[CONV_TAIL:CONV_TAIL + tm, :] = u
        s1 = u_ref[CONV_TAIL - 1:CONV_TAIL - 1 + tm, :]
        s2 = u_ref[CONV_TAIL - 2:CONV_TAIL - 2 + tm, :]
        conv_ref[...] = cw[0:1, :] * s2 + cw[1:2, :] * s1 + cw[2:3, :] * u
        tail = u[tm - CONV_TAIL:tm, :]
        u_ref[0:CONV_TAIL, :] = tail
        tail_ref[...] = tail
    else:
        row = lax.broadcasted_iota(jnp.int32, (tm, d), 0) % seg
        s1 = jnp.where(row == 0, p1_ref[...], pltpu.roll(u, 1, 0))
        s2 = jnp.where(row < 2, p2_ref[...], pltpu.roll(u, 2, 0))
        conv_ref[...] = cw[0:1, :] * s2 + cw[1:2, :] * s1 + cw[2:3, :] * u
        tail_ref[...] = u

    conv_out = proj(0) * conv_ref[...]
    mc_ref[...] = (_sigmoid(proj(6)) * conv_out).astype(_BF16)
    ga_ref[...] = _sigmoid(proj(7)).astype(_BF16)
    q_ref[...] = (proj(3) * scale).astype(_BF16)
    k = proj(4)
    k_ref[...] = k
    kb_ref[...] = k.astype(_BF16)
    v = proj(5)
    v_ref[...] = v
    vb_ref[...] = v.astype(_BF16)


def _project(x, g, w_bf, conv_w, *, tm, seg, prev=None):
    bsz, t, d = x.shape
    carried = prev is None
    n_t = t // tm if carried else 1
    rows = bsz * t
    x2 = x.reshape(rows, d)
    head_dim = d // 16
    scale = float(head_dim) ** -0.5

    if carried:
        grid = (bsz, n_t)
        row_map = lambda b, j: (b * n_t + j, 0)
    else:
        assert rows == tm
        grid = (1, 1)
        row_map = lambda b, j: (0, 0)
    const = lambda b, j: (0, 0)
    row_spec = pl.BlockSpec((tm, d), row_map)
    in_specs = [row_spec,
                pl.BlockSpec((1, d), const),
                pl.BlockSpec((d, N_PROJ_GROUPS * d), const, pipeline_mode=pl.Buffered(1)),
                pl.BlockSpec((CONV_WIDTH, d), const)]
    args = [x2, g.reshape(1, d), w_bf, conv_w]
    if not carried:
        in_specs += [row_spec, row_spec]
        args += [prev[0], prev[1]]
    bf = jax.ShapeDtypeStruct((rows, d), _BF16)
    f32 = jax.ShapeDtypeStruct((rows, d), _F32)
    if carried:
        tail_shape = jax.ShapeDtypeStruct((bsz, CONV_TAIL, d), _F32)
        tail_spec = pl.BlockSpec((None, CONV_TAIL, d), lambda b, j: (b, 0, 0))
    else:
        tail_shape = f32
        tail_spec = row_spec
    out_shape = [bf, f32, f32, bf, bf, bf, bf, tail_shape]
    out_specs = [row_spec] * 7 + [tail_spec]
    return pl.pallas_call(
        functools.partial(_proj_kernel, d=d, tm=tm, seg=seg, scale=scale),
        grid=grid, in_specs=in_specs, out_specs=out_specs, out_shape=out_shape,
        scratch_shapes=[pltpu.VMEM((tm, d), _BF16),
                        pltpu.VMEM((tm + CONV_TAIL, d), _F32),
                        pltpu.VMEM((tm, d), _F32)],
        compiler_params=pltpu.CompilerParams(
            dimension_semantics=("arbitrary", "arbitrary"), vmem_limit_bytes=VMEM_LIMIT_BYTES),
        name="proj_carried" if carried else "proj_segmented",
    )(*args)


def _attend_block(q_ref, kblk, vblk, tt_ref, carry_ref, oacc_ref, mask):
    d = q_ref.shape[1]
    lane = lax.broadcasted_iota(jnp.int32, (KEY_BLOCK, LANES), 1)
    first_head = lane < LANES // HEADS_PER_SLAB
    tt = tt_ref[...]
    smallest = None
    for p in range(d // LANES):
        sl = slice(p * LANES, (p + 1) * LANES)
        k2 = kblk[:, sl]
        v2 = vblk[:, sl]
        zero = jnp.zeros_like(k2)
        kbd = jnp.concatenate([jnp.where(first_head, k2, zero), jnp.where(first_head, zero, k2)], axis=0)
        vbd = jnp.concatenate([jnp.where(first_head, v2, zero), jnp.where(first_head, zero, v2)], axis=0)
        s = lax.dot_general(q_ref[:, sl], kbd, (((1,), (1,)), ((), ())), preferred_element_type=_F32)
        soft = jnp.maximum(s, 0.0) + jnp.log(1.0 + jnp.exp(-jnp.abs(s)))
        if mask is not None:
            soft = jnp.where(mask, soft, 0.0)
        hi = soft.astype(_BF16)
        lo = (soft - hi.astype(_F32)).astype(_BF16)
        cr0 = jnp.dot(jnp.concatenate([hi[:, :KEY_BLOCK], lo[:, :KEY_BLOCK]], axis=1), tt,
                      preferred_element_type=_F32)
        cr1 = jnp.dot(jnp.concatenate([hi[:, KEY_BLOCK:], lo[:, KEY_BLOCK:]], axis=1), tt,
                      preferred_element_type=_F32)
        csum = jnp.concatenate([cr0[:, :KEY_BLOCK], cr1[:, :KEY_BLOCK]], axis=1)
        rsum = jnp.concatenate([cr0[:, KEY_BLOCK:], cr1[:, KEY_BLOCK:]], axis=1)
        carry = carry_ref[p]
        a = jnp.exp(s - csum - carry)
        if mask is not None:
            a = jnp.where(mask, a, 0.0)
        oacc_ref[:, sl] += jnp.dot(a.astype(_BF16), vbd, preferred_element_type=_F32)
        carry = carry + rsum
        carry_ref[p] = carry
        smallest = carry if smallest is None else jnp.minimum(smallest, carry)
    return jnp.min(smallest)


def _causal_mask(tq):
    row = lax.broadcasted_iota(jnp.int32, (tq, KEY_BLOCK), 0)
    col = lax.broadcasted_iota(jnp.int32, (tq, KEY_BLOCK), 1)
    m = col < row
    return jnp.concatenate([m] * HEADS_PER_SLAB, axis=1)


def _walk_back(first_block, smallest, load_block, q_ref, tt_ref, carry_ref, oacc_ref):
    def cond(state):
        kb, m = state
        return jnp.logical_and(kb >= 0, m <= CARRY_DONE)

    def body(state):
        kb, _ = state
        kblk, vblk = load_block(kb)
        m = _attend_block(q_ref, kblk, vblk, tt_ref, carry_ref, oacc_ref, None)
        return kb - 1, m

    lax.while_loop(cond, body, (jnp.asarray(first_block, jnp.int32), smallest))


def _attn_prompt_kernel(q_ref, k_ref, v_ref, tt_ref, o_ref, carry_ref, oacc_ref):
    i = pl.program_id(1)
    tq = q_ref.shape[0]
    carry_ref[...] = jnp.zeros(carry_ref.shape, _F32)
    oacc_ref[...] = jnp.zeros(oacc_ref.shape, _F32)

    def load_block(kb):
        start = pl.multiple_of(kb * KEY_BLOCK, KEY_BLOCK)
        return k_ref[pl.ds(start, KEY_BLOCK), :], v_ref[pl.ds(start, KEY_BLOCK), :]

    kblk, vblk = load_block(i)
    m = _attend_block(q_ref, kblk, vblk, tt_ref, carry_ref, oacc_ref, _causal_mask(tq))
    _walk_back(i - 1, m, load_block, q_ref, tt_ref, carry_ref, oacc_ref)
    o_ref[...] = oacc_ref[...].astype(_BF16)


def _attn_sample_kernel(q_ref, kn_ref, vn_ref, kc_ref, vc_ref, tt_ref, o_ref, carry_ref, oacc_ref):
    tq, d = q_ref.shape
    past = kc_ref.shape[0]
    carry_ref[...] = jnp.zeros(carry_ref.shape, _F32)
    oacc_ref[...] = jnp.zeros(oacc_ref.shape, _F32)
    pad = jnp.zeros((KEY_BLOCK - tq, d), _BF16)
    kblk = jnp.concatenate([kn_ref[...], pad], axis=0)
    vblk = jnp.concatenate([vn_ref[...], pad], axis=0)
    m = _attend_block(q_ref, kblk, vblk, tt_ref, carry_ref, oacc_ref, _causal_mask(tq))

    def load_block(kb):
        start = pl.multiple_of(kb * KEY_BLOCK, KEY_BLOCK)
        return (kc_ref[pl.ds(start, KEY_BLOCK), :].astype(_BF16),
                vc_ref[pl.ds(start, KEY_BLOCK), :].astype(_BF16))

    _walk_back(past // KEY_BLOCK - 1, m, load_block, q_ref, tt_ref, carry_ref, oacc_ref)
    o_ref[...] = oacc_ref[...].astype(_BF16)


def _cumsum_matrix():
    j = lax.broadcasted_iota(jnp.int32, (KEY_BLOCK, KEY_BLOCK), 0)
    s = lax.broadcasted_iota(jnp.int32, (KEY_BLOCK, KEY_BLOCK), 1)
    half = jnp.concatenate([(j >= s).astype(_BF16), jnp.ones((KEY_BLOCK, KEY_BLOCK), _BF16)], axis=1)
    return jnp.concatenate([half, half], axis=0)


def _attn_scratch(tq, d):
    return [pltpu.VMEM((d // LANES, tq, HEADS_PER_SLAB * KEY_BLOCK), _F32), pltpu.VMEM((tq, d), _F32)]


def _attention_prompt(q, kb, vb, bsz, seq, *, tq):
    d = q.shape[1]
    n_q = seq // tq
    tt = _cumsum_matrix()
    seq_spec = pl.BlockSpec((seq, d), lambda b, i: (b, 0))
    return pl.pallas_call(
        _attn_prompt_kernel,
        grid=(bsz, n_q),
        in_specs=[pl.BlockSpec((tq, d), lambda b, i: (b * n_q + i, 0)), seq_spec, seq_spec,
                  pl.BlockSpec(tt.shape, lambda b, i: (0, 0))],
        out_specs=pl.BlockSpec((tq, d), lambda b, i: (b * n_q + i, 0)),
        out_shape=jax.ShapeDtypeStruct(q.shape, _BF16),
        scratch_shapes=_attn_scratch(tq, d),
        compiler_params=pltpu.CompilerParams(
            dimension_semantics=("arbitrary", "arbitrary"), vmem_limit_bytes=VMEM_LIMIT_BYTES),
        name="attn_prompt",
    )(q, kb, vb, tt)


def _attention_sample(q, kb_new, vb_new, cache_k, cache_v, bsz, tq):
    d = q.shape[1]
    past = cache_k.shape[1]
    tt = _cumsum_matrix()
    new_spec = pl.BlockSpec((tq, d), lambda b: (b, 0))
    cache_spec = pl.BlockSpec((None, past, d), lambda b: (b, 0, 0))
    return pl.pallas_call(
        _attn_sample_kernel,
        grid=(bsz,),
        in_specs=[new_spec, new_spec, new_spec, cache_spec, cache_spec,
                  pl.BlockSpec(tt.shape, lambda b: (0, 0))],
        out_specs=new_spec,
        out_shape=jax.ShapeDtypeStruct(q.shape, _BF16),
        scratch_shapes=_attn_scratch(tq, d),
        compiler_params=pltpu.CompilerParams(
            dimension_semantics=("arbitrary",), vmem_limit_bytes=VMEM_LIMIT_BYTES),
        name="attn_sample",
    )(q, kb_new, vb_new, cache_k, cache_v, tt)


def _ffn_kernel(x_ref, mc_ref, ga_ref, at_ref, wo_ref, gf_ref, wu_ref, wd_ref, gl_ref, y_ref,
                h_ref, hn_ref, *, f_chunk, final_norm):
    mixed = mc_ref[...] + ga_ref[...] * at_ref[...]
    h = x_ref[...] + jnp.dot(mixed, wo_ref[...], preferred_element_type=_F32)
    h_ref[...] = h
    hn_ref[...] = (h * _rms_scale(h) * gf_ref[...]).astype(_BF16)
    d_ff = wu_ref.shape[1]
    acc = h_ref[...]
    for c in range(d_ff // f_chunk):
        cs = slice(c * f_chunk, (c + 1) * f_chunk)
        up = jnp.dot(hn_ref[...], wu_ref[:, cs], preferred_element_type=_F32)
        act = jnp.square(jnp.maximum(up, 0.0)).astype(_BF16)
        acc = acc + jnp.dot(act, wd_ref[cs, :], preferred_element_type=_F32)
    y_ref[...] = acc * _rms_scale(acc) * gl_ref[...] if final_norm else acc


def _merge_ffn(x2, mc, ga, attn, wo_bf, g_ffn, wu_bf, wd_bf, g_final, *, tm, final_norm):
    rows, d = x2.shape
    d_ff = wu_bf.shape[1]
    row_spec = pl.BlockSpec((tm, d), lambda i: (i, 0))
    const = lambda i: (0, 0)
    once = pl.Buffered(1)
    return pl.pallas_call(
        functools.partial(_ffn_kernel, f_chunk=d, final_norm=final_norm),
        grid=(rows // tm,),
        in_specs=[row_spec, row_spec, row_spec, row_spec,
                  pl.BlockSpec((d, d), const, pipeline_mode=once),
                  pl.BlockSpec((1, d), const),
                  pl.BlockSpec((d, d_ff), const, pipeline_mode=once),
                  pl.BlockSpec((d_ff, d), const, pipeline_mode=once),
                  pl.BlockSpec((1, d), const)],
        out_specs=row_spec,
        out_shape=jax.ShapeDtypeStruct((rows, d), _F32),
        scratch_shapes=[pltpu.VMEM((tm, d), _F32), pltpu.VMEM((tm, d), _BF16)],
        compiler_params=pltpu.CompilerParams(
            dimension_semantics=("arbitrary",), vmem_limit_bytes=VMEM_LIMIT_BYTES),
        name="merge_ffn",
    )(x2, mc, ga, attn, wo_bf, g_ffn.reshape(1, d), wu_bf, wd_bf, g_final.reshape(1, d))


def _row_tile(rows, limit=512):
    tm = min(rows, limit)
    assert rows % tm == 0 and tm % CONV_TAIL == 0
    return tm


def kernel(x_prompt, x_sample, cache_conv, cache_k, cache_v, g_mix, w_in, conv_w, w_out,
           g_ffn, w_up, w_down, g_final):
    depth = w_in.shape[0]
    bsz, seq, d = x_prompt.shape
    dbsz, dseq, _ = x_sample.shape
    past, n_heads, head_dim = cache_k.shape[2], cache_k.shape[3], cache_k.shape[4]
    assert n_heads * head_dim == d and head_dim * HEADS_PER_SLAB == LANES
    assert seq % KEY_BLOCK == 0 and past % KEY_BLOCK == 0 and dseq <= KEY_BLOCK and dseq >= CONV_WIDTH - 1

    xp, xs = x_prompt, x_sample
    outs = {name: [] for name in ("conv_p", "k_p", "v_p", "conv_s", "k_s", "v_s")}
    for l in range(depth):
        w_bf = w_in[l].astype(_BF16)
        wo_bf = w_out[l].astype(_BF16)
        wu_bf = w_up[l].astype(_BF16)
        wd_bf = w_down[l].astype(_BF16)

        tm = _row_tile(seq)
        q, k, v, kb, vb, mc, ga, tail = _project(xp, g_mix[l], w_bf, conv_w[l], tm=tm, seg=tm)
        attn = _attention_prompt(q, kb, vb, bsz, seq, tq=KEY_BLOCK)
        outs["conv_p"].append(tail[:, CONV_TAIL - (CONV_WIDTH - 1):, :])
        outs["k_p"].append(k.reshape(bsz, seq, n_heads, head_dim))
        outs["v_p"].append(v.reshape(bsz, seq, n_heads, head_dim))
        xp = _merge_ffn(xp.reshape(bsz * seq, d), mc, ga, attn, wo_bf, g_ffn[l], wu_bf, wd_bf,
                        g_final, tm=tm, final_norm=l == depth - 1).reshape(bsz, seq, d)

        rows = dbsz * dseq
        hist = cache_conv[l].astype(_F32)
        prev1 = jnp.zeros((dbsz, dseq, d), _F32).at[:, 0].set(hist[:, 1]).reshape(rows, d)
        prev2 = (jnp.zeros((dbsz, dseq, d), _F32).at[:, 0].set(hist[:, 0]).at[:, 1].set(hist[:, 1])
                 .reshape(rows, d))
        q, k, v, kb, vb, mc, ga, u = _project(xs, g_mix[l], w_bf, conv_w[l], tm=rows, seg=dseq,
                                             prev=(prev1, prev2))
        attn = _attention_sample(q, kb, vb, cache_k[l].reshape(dbsz, past, d),
                                 cache_v[l].reshape(dbsz, past, d), dbsz, dseq)
        outs["conv_s"].append(u.reshape(dbsz, dseq, d)[:, dseq - (CONV_WIDTH - 1):, :])
        outs["k_s"].append(k.reshape(dbsz, dseq, n_heads, head_dim))
        outs["v_s"].append(v.reshape(dbsz, dseq, n_heads, head_dim))
        xs = _merge_ffn(xs.reshape(rows, d), mc, ga, attn, wo_bf, g_ffn[l], wu_bf, wd_bf,
                        g_final, tm=_row_tile(rows), final_norm=l == depth - 1).reshape(dbsz, dseq, d)

    return (xp, xs, jnp.stack(outs["conv_p"]), jnp.stack(outs["k_p"]), jnp.stack(outs["v_p"]),
            jnp.stack(outs["conv_s"]), jnp.stack(outs["k_s"]), jnp.stack(outs["v_s"]))
```

```python
import functools

import jax
import jax.numpy as jnp
from jax import lax
from jax.experimental import pallas as pl
from jax.experimental.pallas import tpu as pltpu

RMS_EPS = 1e-6
CONV_WIDTH = 3
N_PROJ_GROUPS = 8
LANES = 128
KEY_BLOCK = 128
HEADS_PER_SLAB = 2
TILES_IN_FLIGHT = 3
SLABS_PER_CUMSUM = 8
CONV_TAIL = 8
VMEM_LIMIT_BYTES = 60 * 1024 * 1024
CARRY_DONE = 104.0

_F32 = jnp.float32
_BF16 = jnp.bfloat16


def _rms_scale(x32):
    return lax.rsqrt(jnp.mean(x32 * x32, axis=-1, keepdims=True) + RMS_EPS)


def _sigmoid(z):
    return 1.0 / (1.0 + jnp.exp(-z))


def _proj_kernel(*refs, d, tm, seg, scale):
    carried = seg == tm
    if carried:
        (x_ref, g_ref, w_ref, cw_ref,
         q_ref, k_ref, v_ref, kb_ref, vb_ref, mc_ref, ga_ref, tail_ref,
         xn_ref, u_ref, conv_ref) = refs
    else:
        (x_ref, g_ref, w_ref, cw_ref, p1_ref, p2_ref,
         q_ref, k_ref, v_ref, kb_ref, vb_ref, mc_ref, ga_ref, tail_ref,
         xn_ref, u_ref, conv_ref) = refs

    x32 = x_ref[...]
    xn_ref[...] = (x32 * _rms_scale(x32) * g_ref[...]).astype(_BF16)

    def proj(group):
        return jnp.dot(xn_ref[...], w_ref[:, group * d:(group + 1) * d], preferred_element_type=_F32)

    u = proj(1) * proj(2)
    cw = cw_ref[...]
    if carried:
        j = pl.program_id(1)

        @pl.when(j == 0)
        def _():
            u_ref[0:CONV_TAIL, :] = jnp.zeros((CONV_TAIL, d), _F32)

        u_ref[CONV_TAIL:CONV_TAIL + tm, :] = u
        s1 = u_ref[CONV_TAIL - 1:CONV_TAIL - 1 + tm, :]
        s2 = u_ref[CONV_TAIL - 2:CONV_TAIL - 2 + tm, :]
        conv_ref[...] = cw[0:1, :] * s2 + cw[1:2, :] * s1 + cw[2:3, :] * u
        tail = u[tm - CONV_TAIL:tm, :]
        u_ref[0:CONV_TAIL, :] = tail
        tail_ref[...] = tail
    else:
        row = lax.broadcasted_iota(jnp.int32, (tm, d), 0) % seg
        s1 = jnp.where(row == 0, p1_ref[...], pltpu.roll(u, 1, 0))
        s2 = jnp.where(row < 2, p2_ref[...], pltpu.roll(u, 2, 0))
        conv_ref[...] = cw[0:1, :] * s2 + cw[1:2, :] * s1 + cw[2:3, :] * u
        tail_ref[...] = u

    conv_out = proj(0) * conv_ref[...]
    mc_ref[...] = (_sigmoid(proj(6)) * conv_out).astype(_BF16)
    ga_ref[...] = _sigmoid(proj(7)).astype(_BF16)
    q_ref[...] = (proj(3) * scale).astype(_BF16)
    k = proj(4)
    k_ref[...] = k.reshape(k_ref.shape)
    kb_ref[...] = k.astype(_BF16)
    v = proj(5)
    v_ref[...] = v.reshape(v_ref.shape)
    vb_ref[...] = v.astype(_BF16)


def _project(x, g, w_bf, conv_w, *, tm, seg, n_heads, prev=None):
    bsz, t, d = x.shape
    carried = prev is None
    n_t = t // tm if carried else 1
    rows = bsz * t
    x2 = x.reshape(rows, d)
    head_dim = d // n_heads
    scale = float(head_dim) ** -0.5

    if carried:
        grid = (bsz, n_t)
        row_map = lambda b, j: (b * n_t + j, 0)
    else:
        assert rows == tm
        grid = (1, 1)
        row_map = lambda b, j: (0, 0)
    const = lambda b, j: (0, 0)
    row_spec = pl.BlockSpec((tm, d), row_map)
    in_specs = [row_spec,
                pl.BlockSpec((1, d), const),
                pl.BlockSpec((d, N_PROJ_GROUPS * d), const, pipeline_mode=pl.Buffered(1)),
                pl.BlockSpec((CONV_WIDTH, d), const)]
    args = [x2, g.reshape(1, d), w_bf, conv_w]
    if not carried:
        in_specs += [row_spec, row_spec]
        args += [prev[0], prev[1]]
    bf = jax.ShapeDtypeStruct((rows, d), _BF16)
    heads = jax.ShapeDtypeStruct((rows, n_heads, head_dim), _F32)
    heads_spec = pl.BlockSpec((tm, n_heads, head_dim), lambda b, j: row_map(b, j) + (0,))
    if carried:
        tail_shape = jax.ShapeDtypeStruct((bsz, CONV_TAIL, d), _F32)
        tail_spec = pl.BlockSpec((None, CONV_TAIL, d), lambda b, j: (b, 0, 0))
    else:
        tail_shape = jax.ShapeDtypeStruct((rows, d), _F32)
        tail_spec = row_spec
    out_shape = [bf, heads, heads, bf, bf, bf, bf, tail_shape]
    out_specs = [row_spec, heads_spec, heads_spec] + [row_spec] * 4 + [tail_spec]
    return pl.pallas_call(
        functools.partial(_proj_kernel, d=d, tm=tm, seg=seg, scale=scale),
        grid=grid, in_specs=in_specs, out_specs=out_specs, out_shape=out_shape,
        scratch_shapes=[pltpu.VMEM((tm, d), _BF16),
                        pltpu.VMEM((tm + CONV_TAIL, d), _F32),
                        pltpu.VMEM((tm, d), _F32)],
        compiler_params=pltpu.CompilerParams(
            dimension_semantics=("arbitrary", "arbitrary"), vmem_limit_bytes=VMEM_LIMIT_BYTES),
        name="proj_carried" if carried else "proj_segmented",
    )(*args)


def _rows(x, r, tq):
    return x[r * tq:(r + 1) * tq]


def _attend(tiles, kblk, vblk, tt_ref, carry_ref, oacc_ref, s_ref, hl_ref):
    n = len(tiles)
    tq, d = tiles[0][0].shape
    m = n * tq
    n_slabs = d // LANES
    lane = lax.broadcasted_iota(jnp.int32, (KEY_BLOCK, LANES), 1)
    first_head = lane < LANES // HEADS_PER_SLAB

    def per_tile(x, fn):
        return jnp.concatenate([fn(_rows(x, r, tq), tiles[r][2]) for r in range(n)], axis=0) if n > 1 \
            else fn(x, tiles[0][2])

    def masked(x):
        return per_tile(x, lambda piece, mask: piece if mask is None else jnp.where(mask, piece, 0.0))

    def scores_and_softplus(p):
        sl = slice(p * LANES, (p + 1) * LANES)
        k2 = kblk[:, sl]
        kbd = jnp.concatenate([jnp.where(first_head, k2, jnp.zeros_like(k2)),
                               jnp.where(first_head, jnp.zeros_like(k2), k2)], axis=0)
        q2 = jnp.concatenate([t[0][:, sl] for t in tiles], axis=0) if n > 1 else tiles[0][0][:, sl]
        s = lax.dot_general(q2, kbd, (((1,), (1,)), ((), ())), preferred_element_type=_F32)
        s_ref[p, 0:m, :] = s
        neg_abs = lax.bitcast_convert_type(
            lax.bitcast_convert_type(s, jnp.uint32) | jnp.uint32(0x80000000), _F32)
        soft = masked(jnp.maximum(s, 0.0) + jnp.log(1.0 + jnp.exp(neg_abs)))
        hi = soft.astype(_BF16)
        lo = (soft - hi.astype(_F32)).astype(_BF16)
        for h in range(HEADS_PER_SLAB):
            hs = slice(h * KEY_BLOCK, (h + 1) * KEY_BLOCK)
            r0 = (HEADS_PER_SLAB * p + h) * m
            hl_ref[r0:r0 + m, :] = jnp.concatenate([hi[:, hs], lo[:, hs]], axis=1)

    def weights_and_values(p, cr):
        sl = slice(p * LANES, (p + 1) * LANES)
        v2 = vblk[:, sl]
        vbd = jnp.concatenate([jnp.where(first_head, v2, jnp.zeros_like(v2)),
                               jnp.where(first_head, jnp.zeros_like(v2), v2)], axis=0)
        per_head = [cr[h * m:(h + 1) * m] for h in range(HEADS_PER_SLAB)]
        csum = jnp.concatenate([c[:, :KEY_BLOCK] for c in per_head], axis=1)
        rsum = jnp.concatenate([c[:, KEY_BLOCK:] for c in per_head], axis=1)
        carry = jnp.concatenate([carry_ref[t[1], p] for t in tiles], axis=0) if n > 1 \
            else carry_ref[tiles[0][1], p]
        a = masked(jnp.exp(s_ref[p, 0:m, :] - csum - carry))
        o = jnp.dot(a.astype(_BF16), vbd, preferred_element_type=_F32)
        carry = carry + rsum
        for r, (_, slot, _) in enumerate(tiles):
            oacc_ref[slot, :, sl] += _rows(o, r, tq)
            carry_ref[slot, p] = _rows(carry, r, tq)
        return _rows(carry, n - 1, tq)

    smallest = None
    rows_per_slab = HEADS_PER_SLAB * m
    for p in range(n_slabs):
        scores_and_softplus(p)
    crs = [jnp.dot(hl_ref[g0 * rows_per_slab:(g0 + SLABS_PER_CUMSUM) * rows_per_slab, :], tt_ref[...],
                   preferred_element_type=_F32) for g0 in range(0, n_slabs, SLABS_PER_CUMSUM)]
    for p in range(n_slabs):
        g, i = divmod(p, SLABS_PER_CUMSUM)
        last = weights_and_values(p, crs[g][i * rows_per_slab:(i + 1) * rows_per_slab])
        smallest = last if smallest is None else jnp.minimum(smallest, last)
    return jnp.min(smallest)


def _causal_mask(tq):
    row = lax.broadcasted_iota(jnp.int32, (tq, KEY_BLOCK), 0)
    col = lax.broadcasted_iota(jnp.int32, (tq, KEY_BLOCK), 1)
    m = col < row
    return jnp.concatenate([m] * HEADS_PER_SLAB, axis=1)


def _walk_back(first_block, smallest, load_block, tile, tt_ref, carry_ref, oacc_ref, s_ref, hl_ref):
    def cond(state):
        kb, m = state
        return jnp.logical_and(kb >= 0, m <= CARRY_DONE)

    def body(state):
        kb, _ = state
        kblk, vblk = load_block(kb)
        m = _attend([tile], kblk, vblk, tt_ref, carry_ref, oacc_ref, s_ref, hl_ref)
        return kb - 1, m

    lax.while_loop(cond, body, (jnp.asarray(first_block, jnp.int32), smallest))


def _attn_prompt_kernel(q0_ref, q1_ref, q2_ref, k_ref, v_ref, tt_ref, o_ref,
                        carry_ref, oacc_ref, s_ref, hl_ref, m_ref, *, n_blocks):
    j = n_blocks - 1 - pl.program_id(1)
    tq, d = q0_ref.shape
    slots = [lax.rem(j + TILES_IN_FLIGHT + r, TILES_IN_FLIGHT) for r in range(TILES_IN_FLIGHT)]

    def load_block(kb):
        start = pl.multiple_of(kb * KEY_BLOCK, KEY_BLOCK)
        return k_ref[pl.ds(start, KEY_BLOCK), :], v_ref[pl.ds(start, KEY_BLOCK), :]

    @pl.when(j >= 0)
    def _():
        carry_ref[slots[0]] = jnp.zeros(carry_ref.shape[1:], _F32)
        oacc_ref[slots[0]] = jnp.zeros(oacc_ref.shape[1:], _F32)
        kblk, vblk = load_block(j)
        tiles = [(q0_ref, slots[0], _causal_mask(tq)), (q1_ref, slots[1], None), (q2_ref, slots[2], None)]
        m_ref[0] = _attend(tiles, kblk, vblk, tt_ref, carry_ref, oacc_ref, s_ref, hl_ref)

    first_block = jnp.where(j + TILES_IN_FLIGHT - 1 < n_blocks, j - 1, -1)
    _walk_back(first_block, m_ref[0], load_block, (q2_ref, slots[2], None),
               tt_ref, carry_ref, oacc_ref, s_ref, hl_ref)
    o_ref[...] = oacc_ref[slots[2]].astype(_BF16)


def _attn_sample_kernel(q_ref, kn_ref, vn_ref, kc_ref, vc_ref, tt_ref, o_ref,
                        carry_ref, oacc_ref, s_ref, hl_ref):
    tq, d = q_ref.shape
    past = kc_ref.shape[0]
    carry_ref[...] = jnp.zeros(carry_ref.shape, _F32)
    oacc_ref[...] = jnp.zeros(oacc_ref.shape, _F32)
    pad = jnp.zeros((KEY_BLOCK - tq, d), _BF16)
    kblk = jnp.concatenate([kn_ref[...], pad], axis=0)
    vblk = jnp.concatenate([vn_ref[...], pad], axis=0)
    m = _attend([(q_ref, 0, _causal_mask(tq))], kblk, vblk, tt_ref, carry_ref, oacc_ref, s_ref, hl_ref)

    def load_block(kb):
        start = pl.multiple_of(kb * KEY_BLOCK, KEY_BLOCK)
        return (kc_ref[pl.ds(start, KEY_BLOCK)].reshape(KEY_BLOCK, d).astype(_BF16),
                vc_ref[pl.ds(start, KEY_BLOCK)].reshape(KEY_BLOCK, d).astype(_BF16))

    _walk_back(past // KEY_BLOCK - 1, m, load_block, (q_ref, 0, None),
               tt_ref, carry_ref, oacc_ref, s_ref, hl_ref)
    o_ref[...] = oacc_ref[0].astype(_BF16)


def _cumsum_matrix():
    j = lax.broadcasted_iota(jnp.int32, (KEY_BLOCK, KEY_BLOCK), 0)
    s = lax.broadcasted_iota(jnp.int32, (KEY_BLOCK, KEY_BLOCK), 1)
    half = jnp.concatenate([(j >= s).astype(_BF16), jnp.ones((KEY_BLOCK, KEY_BLOCK), _BF16)], axis=1)
    return jnp.concatenate([half, half], axis=0)


def _attn_scratch(n_tiles, tq, d):
    n_slabs = d // LANES
    width = HEADS_PER_SLAB * KEY_BLOCK
    return [pltpu.VMEM((n_tiles, n_slabs, tq, width), _F32),
            pltpu.VMEM((n_tiles, tq, d), _F32),
            pltpu.VMEM((n_slabs, n_tiles * tq, width), _F32),
            pltpu.VMEM((HEADS_PER_SLAB * n_slabs * n_tiles * tq, width), _BF16)]


def _attention_prompt(q, kb, vb, bsz, seq):
    d = q.shape[1]
    tq = KEY_BLOCK
    n_blocks = seq // tq
    tt = _cumsum_matrix()
    seq_spec = pl.BlockSpec((seq, d), lambda b, g: (b, 0), pipeline_mode=pl.Buffered(1))

    def q_spec(r):
        return pl.BlockSpec((tq, d), lambda b, g: (b * n_blocks + jnp.clip(n_blocks - 1 - g + r, 0, n_blocks - 1), 0))

    return pl.pallas_call(
        functools.partial(_attn_prompt_kernel, n_blocks=n_blocks),
        grid=(bsz, n_blocks + TILES_IN_FLIGHT - 1),
        in_specs=[q_spec(0), q_spec(1), q_spec(2), seq_spec, seq_spec,
                  pl.BlockSpec(tt.shape, lambda b, g: (0, 0))],
        out_specs=q_spec(TILES_IN_FLIGHT - 1),
        out_shape=jax.ShapeDtypeStruct(q.shape, _BF16),
        scratch_shapes=_attn_scratch(TILES_IN_FLIGHT, tq, d) + [pltpu.SMEM((1,), _F32)],
        compiler_params=pltpu.CompilerParams(
            dimension_semantics=("arbitrary", "arbitrary"), vmem_limit_bytes=VMEM_LIMIT_BYTES),
        name="attn_prompt",
    )(q, q, q, kb, vb, tt)


def _attention_sample(q, kb_new, vb_new, cache_k, cache_v, bsz, tq):
    d = q.shape[1]
    _, past, n_heads, head_dim = cache_k.shape
    tt = _cumsum_matrix()
    new_spec = pl.BlockSpec((tq, d), lambda b: (b, 0))
    cache_spec = pl.BlockSpec((None, past, n_heads, head_dim), lambda b: (b, 0, 0, 0))
    return pl.pallas_call(
        _attn_sample_kernel,
        grid=(bsz,),
        in_specs=[new_spec, new_spec, new_spec, cache_spec, cache_spec,
                  pl.BlockSpec(tt.shape, lambda b: (0, 0))],
        out_specs=new_spec,
        out_shape=jax.ShapeDtypeStruct(q.shape, _BF16),
        scratch_shapes=_attn_scratch(1, tq, d),
        compiler_params=pltpu.CompilerParams(
            dimension_semantics=("arbitrary",), vmem_limit_bytes=VMEM_LIMIT_BYTES),
        name="attn_sample",
    )(q, kb_new, vb_new, cache_k, cache_v, tt)


def _ffn_kernel(x_ref, mc_ref, ga_ref, at_ref, wo_ref, gf_ref, wu_ref, wd_ref, gl_ref, y_ref,
                h_ref, hn_ref, *, f_chunk, final_norm):
    mixed = mc_ref[...] + ga_ref[...] * at_ref[...]
    h = x_ref[...] + jnp.dot(mixed, wo_ref[...], preferred_element_type=_F32)
    h_ref[...] = h
    hn_ref[...] = (h * _rms_scale(h) * gf_ref[...]).astype(_BF16)
    d_ff = wu_ref.shape[1]
    acc = h_ref[...]
    for c in range(d_ff // f_chunk):
        cs = slice(c * f_chunk, (c + 1) * f_chunk)
        up = jnp.dot(hn_ref[...], wu_ref[:, cs], preferred_element_type=_F32)
        act = jnp.square(jnp.maximum(up, 0.0)).astype(_BF16)
        acc = acc + jnp.dot(act, wd_ref[cs, :], preferred_element_type=_F32)
    y_ref[...] = acc * _rms_scale(acc) * gl_ref[...] if final_norm else acc


def _merge_ffn(x2, mc, ga, attn, wo_bf, g_ffn, wu_bf, wd_bf, g_final, *, tm, final_norm):
    rows, d = x2.shape
    d_ff = wu_bf.shape[1]
    row_spec = pl.BlockSpec((tm, d), lambda i: (i, 0))
    const = lambda i: (0, 0)
    once = pl.Buffered(1)
    return pl.pallas_call(
        functools.partial(_ffn_kernel, f_chunk=d, final_norm=final_norm),
        grid=(rows // tm,),
        in_specs=[row_spec, row_spec, row_spec, row_spec,
                  pl.BlockSpec((d, d), const, pipeline_mode=once),
                  pl.BlockSpec((1, d), const),
                  pl.BlockSpec((d, d_ff), const, pipeline_mode=once),
                  pl.BlockSpec((d_ff, d), const, pipeline_mode=once),
                  pl.BlockSpec((1, d), const)],
        out_specs=row_spec,
        out_shape=jax.ShapeDtypeStruct((rows, d), _F32),
        scratch_shapes=[pltpu.VMEM((tm, d), _F32), pltpu.VMEM((tm, d), _BF16)],
        compiler_params=pltpu.CompilerParams(
            dimension_semantics=("arbitrary",), vmem_limit_bytes=VMEM_LIMIT_BYTES),
        name="merge_ffn",
    )(x2, mc, ga, attn, wo_bf, g_ffn.reshape(1, d), wu_bf, wd_bf, g_final.reshape(1, d))


def _row_tile(rows, limit=512):
    tm = min(rows, limit)
    assert rows % tm == 0 and tm % CONV_TAIL == 0
    return tm


def kernel(x_prompt, x_sample, cache_conv, cache_k, cache_v, g_mix, w_in, conv_w, w_out,
           g_ffn, w_up, w_down, g_final):
    depth = w_in.shape[0]
    bsz, seq, d = x_prompt.shape
    dbsz, dseq, _ = x_sample.shape
    past, n_heads, head_dim = cache_k.shape[2], cache_k.shape[3], cache_k.shape[4]
    assert n_heads * head_dim == d and head_dim * HEADS_PER_SLAB == LANES
    assert seq % KEY_BLOCK == 0 and seq // KEY_BLOCK >= TILES_IN_FLIGHT and past % KEY_BLOCK == 0
    assert CONV_WIDTH - 1 <= dseq <= KEY_BLOCK

    xp, xs = x_prompt, x_sample
    outs = {name: [] for name in ("conv_p", "k_p", "v_p", "conv_s", "k_s", "v_s")}
    for l in range(depth):
        w_bf = w_in[l].astype(_BF16)
        wo_bf = w_out[l].astype(_BF16)
        wu_bf = w_up[l].astype(_BF16)
        wd_bf = w_down[l].astype(_BF16)

        tm = _row_tile(seq)
        q, k, v, kb, vb, mc, ga, tail = _project(xp, g_mix[l], w_bf, conv_w[l], tm=tm, seg=tm,
                                                 n_heads=n_heads)
        attn = _attention_prompt(q, kb, vb, bsz, seq)
        outs["conv_p"].append(tail[:, CONV_TAIL - (CONV_WIDTH - 1):, :])
        outs["k_p"].append(k.reshape(bsz, seq, n_heads, head_dim))
        outs["v_p"].append(v.reshape(bsz, seq, n_heads, head_dim))
        xp = _merge_ffn(xp.reshape(bsz * seq, d), mc, ga, attn, wo_bf, g_ffn[l], wu_bf, wd_bf,
                        g_final, tm=tm, final_norm=l == depth - 1).reshape(bsz, seq, d)

        rows = dbsz * dseq
        hist = cache_conv[l].astype(_F32)
        prev1 = jnp.zeros((dbsz, dseq, d), _F32).at[:, 0].set(hist[:, 1]).reshape(rows, d)
        prev2 = (jnp.zeros((dbsz, dseq, d), _F32).at[:, 0].set(hist[:, 0]).at[:, 1].set(hist[:, 1])
                 .reshape(rows, d))
        q, k, v, kb, vb, mc, ga, u = _project(xs, g_mix[l], w_bf, conv_w[l], tm=rows, seg=dseq,
                                             n_heads=n_heads, prev=(prev1, prev2))
        attn = _attention_sample(q, kb, vb, cache_k[l], cache_v[l], dbsz, dseq)
        outs["conv_s"].append(u.reshape(dbsz, dseq, d)[:, dseq - (CONV_WIDTH - 1):, :])
        outs["k_s"].append(k.reshape(dbsz, dseq, n_heads, head_dim))
        outs["v_s"].append(v.reshape(dbsz, dseq, n_heads, head_dim))
        xs = _merge_ffn(xs.reshape(rows, d), mc, ga, attn, wo_bf, g_ffn[l], wu_bf, wd_bf,
                        g_final, tm=_row_tile(rows), final_norm=l == depth - 1).reshape(dbsz, dseq, d)

    return (xp, xs, jnp.stack(outs["conv_p"]), jnp.stack(outs["k_p"]), jnp.stack(outs["v_p"]),
            jnp.stack(outs["conv_s"]), jnp.stack(outs["k_s"]), jnp.stack(outs["v_s"]))
```

```python
import functools

import jax
import jax.numpy as jnp
from jax import lax
from jax.experimental import pallas as pl
from jax.experimental.pallas import tpu as pltpu

RMS_EPS = 1e-6
CONV_WIDTH = 3
LANES = 128
KEY_BLOCK = 128
HEADS_PER_SLAB = 2
TILES_IN_FLIGHT = 3
CONV_TAIL = 8
VMEM_LIMIT_BYTES = 60 * 1024 * 1024
CARRY_DONE = 104.0

_F32 = jnp.float32
_BF16 = jnp.bfloat16
_NT = (((1,), (1,)), ((), ()))


def _rms_scale(x32):
    return lax.rsqrt(jnp.mean(x32 * x32, axis=-1, keepdims=True) + RMS_EPS)


def _sigmoid(z):
    return 1.0 / (1.0 + jnp.exp(-z))


def _proj_kernel(*refs, d, tm, seg, scale):
    carried = seg == tm
    if carried:
        (x_ref, g_ref, w_ref, wkv_ref, cw_ref,
         q_ref, k_ref, v_ref, kb_ref, vb_ref, mc_ref, ga_ref, tail_ref,
         xn_ref, u_ref, conv_ref) = refs
        b_g, c_g, h_g, q_g, gc_g, ga_g = range(6)
    else:
        (x_ref, g_ref, w_ref, cw_ref, p1_ref, p2_ref,
         q_ref, k_ref, v_ref, kb_ref, vb_ref, mc_ref, ga_ref, tail_ref,
         xn_ref, u_ref, conv_ref) = refs
        b_g, c_g, h_g, q_g, k_g, v_g, gc_g, ga_g = range(8)

    x32 = x_ref[...]
    xn_ref[...] = (x32 * _rms_scale(x32) * g_ref[...]).astype(_BF16)

    def proj(group):
        return jnp.dot(xn_ref[...], w_ref[:, group * d:(group + 1) * d], preferred_element_type=_F32)

    u = proj(c_g) * proj(h_g)
    cw = cw_ref[...]
    if carried:
        j = pl.program_id(1)

        @pl.when(j == 0)
        def _():
            u_ref[0:CONV_TAIL, :] = jnp.zeros((CONV_TAIL, d), _F32)

        u_ref[CONV_TAIL:CONV_TAIL + tm, :] = u
        s1 = u_ref[CONV_TAIL - 1:CONV_TAIL - 1 + tm, :]
        s2 = u_ref[CONV_TAIL - 2:CONV_TAIL - 2 + tm, :]
        conv_ref[...] = cw[0:1, :] * s2 + cw[1:2, :] * s1 + cw[2:3, :] * u
        tail = u[tm - CONV_TAIL:tm, :]
        u_ref[0:CONV_TAIL, :] = tail
        tail_ref[...] = tail
    else:
        row = lax.broadcasted_iota(jnp.int32, (tm, d), 0) % seg
        s1 = jnp.where(row == 0, p1_ref[...], pltpu.roll(u, 1, 0))
        s2 = jnp.where(row < 2, p2_ref[...], pltpu.roll(u, 2, 0))
        conv_ref[...] = cw[0:1, :] * s2 + cw[1:2, :] * s1 + cw[2:3, :] * u
        tail_ref[...] = u

    conv_out = proj(b_g) * conv_ref[...]
    mc_ref[...] = (_sigmoid(proj(gc_g)) * conv_out).astype(_BF16)
    ga_ref[...] = _sigmoid(proj(ga_g)).astype(_BF16)
    q_ref[...] = (proj(q_g) * scale).astype(_BF16)
    if carried:
        for i, (o_ref, ob_ref) in enumerate(((k_ref, kb_ref), (v_ref, vb_ref))):
            t = lax.dot_general(wkv_ref[i * d:(i + 1) * d, :], xn_ref[...], _NT,
                                preferred_element_type=_F32)
            o_ref[...] = t.reshape(o_ref.shape)
            ob_ref[...] = t.astype(_BF16)
    else:
        for group, o_ref, ob_ref in ((k_g, k_ref, kb_ref), (v_g, v_ref, vb_ref)):
            t = proj(group)
            o_ref[...] = t.reshape(o_ref.shape)
            ob_ref[...] = t.astype(_BF16)


def _project(x, g, w_bf, conv_w, *, tm, seg, n_heads, wkv_t=None, prev=None):
    bsz, t, d = x.shape
    carried = prev is None
    n_t = t // tm if carried else 1
    rows = bsz * t
    x2 = x.reshape(rows, d)
    head_dim = d // n_heads
    scale = float(head_dim) ** -0.5

    if carried:
        grid = (bsz, n_t)
        row_map = lambda b, j: (b * n_t + j, 0)
    else:
        assert rows == tm
        grid = (1, 1)
        row_map = lambda b, j: (0, 0)
    const = lambda b, j: (0, 0)
    once = pl.Buffered(1)
    row_spec = pl.BlockSpec((tm, d), row_map)
    in_specs = [row_spec, pl.BlockSpec((1, d), const), pl.BlockSpec(w_bf.shape, const, pipeline_mode=once)]
    args = [x2, g.reshape(1, d), w_bf]
    if carried:
        in_specs.append(pl.BlockSpec(wkv_t.shape, const, pipeline_mode=once))
        args.append(wkv_t)
    in_specs.append(pl.BlockSpec((CONV_WIDTH, d), const))
    args.append(conv_w)
    bf = jax.ShapeDtypeStruct((rows, d), _BF16)
    if carried:
        kv = jax.ShapeDtypeStruct((bsz, n_heads, head_dim, t), _F32)
        kv_spec = pl.BlockSpec((None, n_heads, head_dim, tm), lambda b, j: (b, 0, 0, j))
        kvb = jax.ShapeDtypeStruct((bsz, d, t), _BF16)
        kvb_spec = pl.BlockSpec((None, d, tm), lambda b, j: (b, 0, j))
        tail_shape = jax.ShapeDtypeStruct((bsz, CONV_TAIL, d), _F32)
        tail_spec = pl.BlockSpec((None, CONV_TAIL, d), lambda b, j: (b, 0, 0))
    else:
        in_specs += [row_spec, row_spec]
        args += [prev[0], prev[1]]
        kv = jax.ShapeDtypeStruct((rows, n_heads, head_dim), _F32)
        kv_spec = pl.BlockSpec((tm, n_heads, head_dim), lambda b, j: (0, 0, 0))
        kvb, kvb_spec = bf, row_spec
        tail_shape = jax.ShapeDtypeStruct((rows, d), _F32)
        tail_spec = row_spec
    out_shape = [bf, kv, kv, kvb, kvb, bf, bf, tail_shape]
    out_specs = [row_spec, kv_spec, kv_spec, kvb_spec, kvb_spec, row_spec, row_spec, tail_spec]
    return pl.pallas_call(
        functools.partial(_proj_kernel, d=d, tm=tm, seg=seg, scale=scale),
        grid=grid, in_specs=in_specs, out_specs=out_specs, out_shape=out_shape,
        scratch_shapes=[pltpu.VMEM((tm, d), _BF16),
                        pltpu.VMEM((tm + CONV_TAIL, d), _F32),
                        pltpu.VMEM((tm, d), _F32)],
        compiler_params=pltpu.CompilerParams(
            dimension_semantics=("arbitrary", "arbitrary"), vmem_limit_bytes=VMEM_LIMIT_BYTES),
        name="proj_carried" if carried else "proj_segmented",
    )(*args)


def _rows(x, r, tq):
    return x[r * tq:(r + 1) * tq]


def _attend(tiles, kblk, vblk, keys_on_lanes, tt_ref, carry_ref, oacc_ref, s_ref, hl_ref):
    n = len(tiles)
    tq, d = tiles[0][0].shape
    m = n * tq
    n_slabs = d // LANES
    half = LANES // HEADS_PER_SLAB

    def block_diagonal(blk, p):
        if keys_on_lanes:
            x = blk[p * LANES:(p + 1) * LANES, :]
            zero = jnp.zeros((half, KEY_BLOCK), x.dtype)
            return jnp.concatenate([jnp.concatenate([x[:half], zero], axis=1),
                                    jnp.concatenate([zero, x[half:]], axis=1)], axis=0)
        x = blk[:, p * LANES:(p + 1) * LANES]
        first_head = lax.broadcasted_iota(jnp.int32, x.shape, 1) < half
        return jnp.concatenate([jnp.where(first_head, x, jnp.zeros_like(x)),
                                jnp.where(first_head, jnp.zeros_like(x), x)], axis=0)

    def per_tile(x, fn):
        return jnp.concatenate([fn(_rows(x, r, tq), tiles[r][2]) for r in range(n)], axis=0) if n > 1 \
            else fn(x, tiles[0][2])

    def masked(x):
        return per_tile(x, lambda piece, mask: piece if mask is None else jnp.where(mask, piece, 0.0))

    def scores_and_softplus(p):
        sl = slice(p * LANES, (p + 1) * LANES)
        kbd = block_diagonal(kblk, p)
        q2 = jnp.concatenate([t[0][:, sl] for t in tiles], axis=0) if n > 1 else tiles[0][0][:, sl]
        if keys_on_lanes:
            s = jnp.dot(q2, kbd, preferred_element_type=_F32)
        else:
            s = lax.dot_general(q2, kbd, _NT, preferred_element_type=_F32)
        s_ref[p, 0:m, :] = s
        soft = masked(jnp.maximum(s, 0.0) + jnp.log(1.0 + jnp.exp(-jnp.abs(s))))
        hi = soft.astype(_BF16)
        lo = (soft - hi.astype(_F32)).astype(_BF16)
        for h in range(HEADS_PER_SLAB):
            hs = slice(h * KEY_BLOCK, (h + 1) * KEY_BLOCK)
            r0 = (HEADS_PER_SLAB * p + h) * m
            hl_ref[r0:r0 + m, :] = jnp.concatenate([hi[:, hs], lo[:, hs]], axis=1)

    def weights_and_values(p, cr):
        sl = slice(p * LANES, (p + 1) * LANES)
        vbd = block_diagonal(vblk, p)
        per_head = [cr[h * m:(h + 1) * m] for h in range(HEADS_PER_SLAB)]
        csum = jnp.concatenate([c[:, :KEY_BLOCK] for c in per_head], axis=1)
        rsum = jnp.concatenate([c[:, KEY_BLOCK:] for c in per_head], axis=1)
        carry = jnp.concatenate([carry_ref[t[1], p] for t in tiles], axis=0) if n > 1 \
            else carry_ref[tiles[0][1], p]
        a = masked(jnp.exp(s_ref[p, 0:m, :] - csum - carry)).astype(_BF16)
        if keys_on_lanes:
            o = lax.dot_general(a, vbd, _NT, preferred_element_type=_F32)
        else:
            o = jnp.dot(a, vbd, preferred_element_type=_F32)
        carry = carry + rsum
        for r, (_, slot, _) in enumerate(tiles):
            oacc_ref[slot, :, sl] += _rows(o, r, tq)
            carry_ref[slot, p] = _rows(carry, r, tq)
        return _rows(carry, n - 1, tq)

    for p in range(n_slabs):
        scores_and_softplus(p)
    rows_per_slab = HEADS_PER_SLAB * m
    cr = jnp.dot(hl_ref[0:n_slabs * rows_per_slab, :], tt_ref[...], preferred_element_type=_F32)
    smallest = None
    for p in range(n_slabs):
        last = weights_and_values(p, cr[p * rows_per_slab:(p + 1) * rows_per_slab])
        smallest = last if smallest is None else jnp.minimum(smallest, last)
    return jnp.min(smallest)


def _causal_mask(tq):
    row = lax.broadcasted_iota(jnp.int32, (tq, KEY_BLOCK), 0)
    col = lax.broadcasted_iota(jnp.int32, (tq, KEY_BLOCK), 1)
    m = col < row
    return jnp.concatenate([m] * HEADS_PER_SLAB, axis=1)


def _walk_back(first_block, smallest, load_block, tile, tt_ref, carry_ref, oacc_ref, s_ref, hl_ref):
    def cond(state):
        kb, m = state
        return jnp.logical_and(kb >= 0, m <= CARRY_DONE)

    def body(state):
        kb, _ = state
        kblk, vblk = load_block(kb)
        m = _attend([tile], kblk, vblk, True, tt_ref, carry_ref, oacc_ref, s_ref, hl_ref)
        return kb - 1, m

    lax.while_loop(cond, body, (jnp.asarray(first_block, jnp.int32), smallest))


def _attn_prompt_kernel(q0_ref, q1_ref, q2_ref, k_ref, v_ref, tt_ref, o_ref,
                        carry_ref, oacc_ref, s_ref, hl_ref, m_ref, *, n_blocks):
    j = n_blocks - 1 - pl.program_id(1)
    tq, d = q0_ref.shape
    slots = [lax.rem(j + TILES_IN_FLIGHT + r, TILES_IN_FLIGHT) for r in range(TILES_IN_FLIGHT)]

    def load_block(kb):
        start = pl.multiple_of(kb * KEY_BLOCK, KEY_BLOCK)
        return k_ref[:, pl.ds(start, KEY_BLOCK)], v_ref[:, pl.ds(start, KEY_BLOCK)]

    @pl.when(j >= 0)
    def _():
        carry_ref[slots[0]] = jnp.zeros(carry_ref.shape[1:], _F32)
        oacc_ref[slots[0]] = jnp.zeros(oacc_ref.shape[1:], _F32)
        kblk, vblk = load_block(j)
        tiles = [(q0_ref, slots[0], _causal_mask(tq)), (q1_ref, slots[1], None), (q2_ref, slots[2], None)]
        m_ref[0] = _attend(tiles, kblk, vblk, True, tt_ref, carry_ref, oacc_ref, s_ref, hl_ref)

    first_block = jnp.where(j + TILES_IN_FLIGHT - 1 < n_blocks, j - 1, -1)
    _walk_back(first_block, m_ref[0], load_block, (q2_ref, slots[2], None),
               tt_ref, carry_ref, oacc_ref, s_ref, hl_ref)
    o_ref[...] = oacc_ref[slots[2]].astype(_BF16)


def _attn_sample_kernel(q_ref, kn_ref, vn_ref, kc_ref, vc_ref, tt_ref, o_ref,
                        carry_ref, oacc_ref, s_ref, hl_ref):
    tq, d = q_ref.shape
    past = kc_ref.shape[1]
    carry_ref[...] = jnp.zeros(carry_ref.shape, _F32)
    oacc_ref[...] = jnp.zeros(oacc_ref.shape, _F32)
    pad = jnp.zeros((KEY_BLOCK - tq, d), _BF16)
    kblk = jnp.concatenate([kn_ref[...], pad], axis=0)
    vblk = jnp.concatenate([vn_ref[...], pad], axis=0)
    m = _attend([(q_ref, 0, _causal_mask(tq))], kblk, vblk, False,
                tt_ref, carry_ref, oacc_ref, s_ref, hl_ref)

    def load_block(kb):
        start = pl.multiple_of(kb * KEY_BLOCK, KEY_BLOCK)
        return (kc_ref[:, pl.ds(start, KEY_BLOCK)].astype(_BF16),
                vc_ref[:, pl.ds(start, KEY_BLOCK)].astype(_BF16))

    _walk_back(past // KEY_BLOCK - 1, m, load_block, (q_ref, 0, None),
               tt_ref, carry_ref, oacc_ref, s_ref, hl_ref)
    o_ref[...] = oacc_ref[0].astype(_BF16)


def _cumsum_matrix():
    j = lax.broadcasted_iota(jnp.int32, (KEY_BLOCK, KEY_BLOCK), 0)
    s = lax.broadcasted_iota(jnp.int32, (KEY_BLOCK, KEY_BLOCK), 1)
    half = jnp.concatenate([(j >= s).astype(_BF16), jnp.ones((KEY_BLOCK, KEY_BLOCK), _BF16)], axis=1)
    return jnp.concatenate([half, half], axis=0)


def _attn_scratch(n_tiles, tq, d):
    n_slabs = d // LANES
    width = HEADS_PER_SLAB * KEY_BLOCK
    return [pltpu.VMEM((n_tiles, n_slabs, tq, width), _F32),
            pltpu.VMEM((n_tiles, tq, d), _F32),
            pltpu.VMEM((n_slabs, n_tiles * tq, width), _F32),
            pltpu.VMEM((HEADS_PER_SLAB * n_slabs * n_tiles * tq, width), _BF16)]


def _attention_prompt(q, kt, vt):
    bsz, d, seq = kt.shape
    tq = KEY_BLOCK
    n_blocks = seq // tq
    tt = _cumsum_matrix()
    seq_spec = pl.BlockSpec((None, d, seq), lambda b, g: (b, 0, 0), pipeline_mode=pl.Buffered(1))

    def q_spec(r):
        return pl.BlockSpec((tq, d), lambda b, g: (b * n_blocks + jnp.clip(n_blocks - 1 - g + r, 0, n_blocks - 1), 0))

    return pl.pallas_call(
        functools.partial(_attn_prompt_kernel, n_blocks=n_blocks),
        grid=(bsz, n_blocks + TILES_IN_FLIGHT - 1),
        in_specs=[q_spec(0), q_spec(1), q_spec(2), seq_spec, seq_spec,
                  pl.BlockSpec(tt.shape, lambda b, g: (0, 0))],
        out_specs=q_spec(TILES_IN_FLIGHT - 1),
        out_shape=jax.ShapeDtypeStruct(q.shape, _BF16),
        scratch_shapes=_attn_scratch(TILES_IN_FLIGHT, tq, d) + [pltpu.SMEM((1,), _F32)],
        compiler_params=pltpu.CompilerParams(
            dimension_semantics=("arbitrary", "arbitrary"), vmem_limit_bytes=VMEM_LIMIT_BYTES),
        name="attn_prompt",
    )(q, q, q, kt, vt, tt)


def _attention_sample(q, kb_new, vb_new, cache_kt, cache_vt, tq):
    bsz, d, past = cache_kt.shape
    tt = _cumsum_matrix()
    new_spec = pl.BlockSpec((tq, d), lambda b: (b, 0))
    cache_spec = pl.BlockSpec((None, d, past), lambda b: (b, 0, 0))
    return pl.pallas_call(
        _attn_sample_kernel,
        grid=(bsz,),
        in_specs=[new_spec, new_spec, new_spec, cache_spec, cache_spec,
                  pl.BlockSpec(tt.shape, lambda b: (0, 0))],
        out_specs=new_spec,
        out_shape=jax.ShapeDtypeStruct(q.shape, _BF16),
        scratch_shapes=_attn_scratch(1, tq, d),
        compiler_params=pltpu.CompilerParams(
            dimension_semantics=("arbitrary",), vmem_limit_bytes=VMEM_LIMIT_BYTES),
        name="attn_sample",
    )(q, kb_new, vb_new, cache_kt, cache_vt, tt)


def _ffn_kernel(x_ref, mc_ref, ga_ref, at_ref, wo_ref, gf_ref, wu_ref, wd_ref, gl_ref, y_ref,
                h_ref, hn_ref, *, f_chunk, final_norm):
    mixed = mc_ref[...] + ga_ref[...] * at_ref[...]
    h = x_ref[...] + jnp.dot(mixed, wo_ref[...], preferred_element_type=_F32)
    h_ref[...] = h
    hn_ref[...] = (h * _rms_scale(h) * gf_ref[...]).astype(_BF16)
    d_ff = wu_ref.shape[1]
    acc = h_ref[...]
    for c in range(d_ff // f_chunk):
        cs = slice(c * f_chunk, (c + 1) * f_chunk)
        up = jnp.dot(hn_ref[...], wu_ref[:, cs], preferred_element_type=_F32)
        act = jnp.square(jnp.maximum(up, 0.0)).astype(_BF16)
        acc = acc + jnp.dot(act, wd_ref[cs, :], preferred_element_type=_F32)
    y_ref[...] = acc * _rms_scale(acc) * gl_ref[...] if final_norm else acc


def _merge_ffn(x2, mc, ga, attn, wo_bf, g_ffn, wu_bf, wd_bf, g_final, *, tm, final_norm):
    rows, d = x2.shape
    d_ff = wu_bf.shape[1]
    row_spec = pl.BlockSpec((tm, d), lambda i: (i, 0))
    const = lambda i: (0, 0)
    once = pl.Buffered(1)
    return pl.pallas_call(
        functools.partial(_ffn_kernel, f_chunk=d, final_norm=final_norm),
        grid=(rows // tm,),
        in_specs=[row_spec, row_spec, row_spec, row_spec,
                  pl.BlockSpec((d, d), const, pipeline_mode=once),
                  pl.BlockSpec((1, d), const),
                  pl.BlockSpec((d, d_ff), const, pipeline_mode=once),
                  pl.BlockSpec((d_ff, d), const, pipeline_mode=once),
                  pl.BlockSpec((1, d), const)],
        out_specs=row_spec,
        out_shape=jax.ShapeDtypeStruct((rows, d), _F32),
        scratch_shapes=[pltpu.VMEM((tm, d), _F32), pltpu.VMEM((tm, d), _BF16)],
        compiler_params=pltpu.CompilerParams(
            dimension_semantics=("arbitrary",), vmem_limit_bytes=VMEM_LIMIT_BYTES),
        name="merge_ffn",
    )(x2, mc, ga, attn, wo_bf, g_ffn.reshape(1, d), wu_bf, wd_bf, g_final.reshape(1, d))


def _row_tile(rows, limit=512):
    tm = min(rows, limit)
    assert rows % tm == 0 and tm % CONV_TAIL == 0
    return tm


def kernel(x_prompt, x_sample, cache_conv, cache_k, cache_v, g_mix, w_in, conv_w, w_out,
           g_ffn, w_up, w_down, g_final):
    depth = w_in.shape[0]
    bsz, seq, d = x_prompt.shape
    dbsz, dseq, _ = x_sample.shape
    past, n_heads, head_dim = cache_k.shape[2], cache_k.shape[3], cache_k.shape[4]
    assert n_heads * head_dim == d and head_dim * HEADS_PER_SLAB == LANES
    assert seq % KEY_BLOCK == 0 and seq // KEY_BLOCK >= TILES_IN_FLIGHT and past % KEY_BLOCK == 0
    assert CONV_WIDTH - 1 <= dseq <= KEY_BLOCK

    def feature_major(c):
        return jnp.transpose(c, (0, 2, 3, 1)).reshape(c.shape[0], d, c.shape[1])

    def position_major(ct):
        return jnp.transpose(ct, (0, 3, 1, 2))

    xp, xs = x_prompt, x_sample
    outs = {name: [] for name in ("conv_p", "k_p", "v_p", "conv_s", "k_s", "v_s")}
    for l in range(depth):
        w_bf = w_in[l].astype(_BF16)
        k0, k1 = 4 * d, 6 * d
        w_rest = jnp.concatenate([w_in[l][:, :k0], w_in[l][:, k1:]], axis=1).astype(_BF16)
        wkv_t = w_in[l][:, k0:k1].T.astype(_BF16)
        wo_bf = w_out[l].astype(_BF16)
        wu_bf = w_up[l].astype(_BF16)
        wd_bf = w_down[l].astype(_BF16)

        tm = _row_tile(seq)
        q, kt, vt, ktb, vtb, mc, ga, tail = _project(xp, g_mix[l], w_rest, conv_w[l], tm=tm, seg=tm,
                                                     n_heads=n_heads, wkv_t=wkv_t)
        attn = _attention_prompt(q, ktb, vtb)
        outs["conv_p"].append(tail[:, CONV_TAIL - (CONV_WIDTH - 1):, :])
        outs["k_p"].append(position_major(kt))
        outs["v_p"].append(position_major(vt))
        xp = _merge_ffn(xp.reshape(bsz * seq, d), mc, ga, attn, wo_bf, g_ffn[l], wu_bf, wd_bf,
                        g_final, tm=tm, final_norm=l == depth - 1).reshape(bsz, seq, d)

        rows = dbsz * dseq
        hist = cache_conv[l].astype(_F32)
        prev1 = jnp.zeros((dbsz, dseq, d), _F32).at[:, 0].set(hist[:, 1]).reshape(rows, d)
        prev2 = (jnp.zeros((dbsz, dseq, d), _F32).at[:, 0].set(hist[:, 0]).at[:, 1].set(hist[:, 1])
                 .reshape(rows, d))
        q, k, v, kb, vb, mc, ga, u = _project(xs, g_mix[l], w_bf, conv_w[l], tm=rows, seg=dseq,
                                             n_heads=n_heads, prev=(prev1, prev2))
        attn = _attention_sample(q, kb, vb, feature_major(cache_k[l]), feature_major(cache_v[l]), dseq)
        outs["conv_s"].append(u.reshape(dbsz, dseq, d)[:, dseq - (CONV_WIDTH - 1):, :])
        outs["k_s"].append(k.reshape(dbsz, dseq, n_heads, head_dim))
        outs["v_s"].append(v.reshape(dbsz, dseq, n_heads, head_dim))
        xs = _merge_ffn(xs.reshape(rows, d), mc, ga, attn, wo_bf, g_ffn[l], wu_bf, wd_bf,
                        g_final, tm=_row_tile(rows), final_norm=l == depth - 1).reshape(dbsz, dseq, d)

    return (xp, xs, jnp.stack(outs["conv_p"]), jnp.stack(outs["k_p"]), jnp.stack(outs["v_p"]),
            jnp.stack(outs["conv_s"]), jnp.stack(outs["k_s"]), jnp.stack(outs["v_s"]))
```

```python
import functools

import jax
import jax.numpy as jnp
from jax import lax
from jax.experimental import pallas as pl
from jax.experimental.pallas import tpu as pltpu

RMS_EPS = 1e-6
CONV_WIDTH = 3
LANES = 128
KEY_BLOCK = 128
HEADS_PER_SLAB = 2
TILES_IN_FLIGHT = 3
LATE_ROWS = 64
CONV_TAIL = 8
VMEM_LIMIT_BYTES = 60 * 1024 * 1024
CARRY_DONE = 104.0
LOG2_E = 1.4426950408889634

_F32 = jnp.float32
_BF16 = jnp.bfloat16
_NT = (((1,), (1,)), ((), ()))


def _rms_scale(x32):
    return lax.rsqrt(jnp.mean(x32 * x32, axis=-1, keepdims=True) + RMS_EPS)


def _sigmoid(z):
    return 1.0 / (1.0 + jnp.exp(-z))


def _proj_kernel(*refs, d, tm, seg, scale):
    carried = seg == tm
    if carried:
        (x_ref, g_ref, w_ref, wkv_ref, cw_ref,
         q_ref, k_ref, v_ref, kb_ref, vb_ref, mc_ref, ga_ref, tail_ref,
         xn_ref, u_ref, conv_ref) = refs
        b_g, c_g, h_g, q_g, gc_g, ga_g = range(6)
    else:
        (x_ref, g_ref, w_ref, cw_ref, p1_ref, p2_ref,
         q_ref, k_ref, v_ref, kb_ref, vb_ref, mc_ref, ga_ref, tail_ref,
         xn_ref, u_ref, conv_ref) = refs
        b_g, c_g, h_g, q_g, k_g, v_g, gc_g, ga_g = range(8)

    if carried:
        @pl.when(pl.program_id(1) == 0)
        def _():
            u_ref[0:CONV_TAIL, :] = jnp.zeros((CONV_TAIL, d), _F32)

    x32 = x_ref[...]
    xn_ref[...] = (x32 * _rms_scale(x32) * g_ref[...]).astype(_BF16)

    def proj(group):
        return jnp.dot(xn_ref[...], w_ref[:, group * d:(group + 1) * d], preferred_element_type=_F32)

    u = proj(c_g) * proj(h_g)
    cw = cw_ref[...]
    if carried:
        u_ref[CONV_TAIL:CONV_TAIL + tm, :] = u
        s1 = u_ref[CONV_TAIL - 1:CONV_TAIL - 1 + tm, :]
        s2 = u_ref[CONV_TAIL - 2:CONV_TAIL - 2 + tm, :]
        conv_ref[...] = cw[0:1, :] * s2 + cw[1:2, :] * s1 + cw[2:3, :] * u
        tail = u[tm - CONV_TAIL:tm, :]
        u_ref[0:CONV_TAIL, :] = tail
        tail_ref[...] = tail
    else:
        row = lax.broadcasted_iota(jnp.int32, (tm, d), 0) % seg
        s1 = jnp.where(row == 0, p1_ref[...], pltpu.roll(u, 1, 0))
        s2 = jnp.where(row < 2, p2_ref[...], pltpu.roll(u, 2, 0))
        conv_ref[...] = cw[0:1, :] * s2 + cw[1:2, :] * s1 + cw[2:3, :] * u
        tail_ref[...] = u

    conv_out = proj(b_g) * conv_ref[...]
    mc_ref[...] = (_sigmoid(proj(gc_g)) * conv_out).astype(_BF16)
    ga_ref[...] = _sigmoid(proj(ga_g)).astype(_BF16)
    q_ref[...] = (proj(q_g) * scale).astype(_BF16)
    if carried:
        for i, (o_ref, ob_ref) in enumerate(((k_ref, kb_ref), (v_ref, vb_ref))):
            t = lax.dot_general(wkv_ref[i * d:(i + 1) * d, :], xn_ref[...], _NT,
                                preferred_element_type=_F32)
            o_ref[...] = t.reshape(o_ref.shape)
            ob_ref[...] = t.astype(_BF16)
    else:
        for group, o_ref, ob_ref in ((k_g, k_ref, kb_ref), (v_g, v_ref, vb_ref)):
            t = proj(group)
            o_ref[...] = t.reshape(o_ref.shape)
            ob_ref[...] = t.astype(_BF16)


def _project(x, g, w_bf, conv_w, *, tm, seg, n_heads, wkv_t=None, prev=None):
    bsz, t, d = x.shape
    carried = prev is None
    n_t = t // tm if carried else 1
    rows = bsz * t
    x2 = x.reshape(rows, d)
    head_dim = d // n_heads
    scale = float(head_dim) ** -0.5

    if carried:
        grid = (bsz, n_t)
        row_map = lambda b, j: (b * n_t + j, 0)
    else:
        assert rows == tm
        grid = (1, 1)
        row_map = lambda b, j: (0, 0)
    const = lambda b, j: (0, 0)
    once = pl.Buffered(1)
    row_spec = pl.BlockSpec((tm, d), row_map)
    in_specs = [row_spec, pl.BlockSpec((1, d), const), pl.BlockSpec(w_bf.shape, const, pipeline_mode=once)]
    args = [x2, g.reshape(1, d), w_bf]
    if carried:
        in_specs.append(pl.BlockSpec(wkv_t.shape, const, pipeline_mode=once))
        args.append(wkv_t)
    in_specs.append(pl.BlockSpec((CONV_WIDTH, d), const))
    args.append(conv_w)
    bf = jax.ShapeDtypeStruct((rows, d), _BF16)
    if carried:
        kv = jax.ShapeDtypeStruct((bsz, n_heads, head_dim, t), _F32)
        kv_spec = pl.BlockSpec((None, n_heads, head_dim, tm), lambda b, j: (b, 0, 0, j))
        kvb = jax.ShapeDtypeStruct((bsz, d, t), _BF16)
        kvb_spec = pl.BlockSpec((None, d, tm), lambda b, j: (b, 0, j))
        tail_shape = jax.ShapeDtypeStruct((bsz, CONV_TAIL, d), _F32)
        tail_spec = pl.BlockSpec((None, CONV_TAIL, d), lambda b, j: (b, 0, 0))
    else:
        in_specs += [row_spec, row_spec]
        args += [prev[0], prev[1]]
        kv = jax.ShapeDtypeStruct((rows, n_heads, head_dim), _F32)
        kv_spec = pl.BlockSpec((tm, n_heads, head_dim), lambda b, j: (0, 0, 0))
        kvb, kvb_spec = bf, row_spec
        tail_shape = jax.ShapeDtypeStruct((rows, d), _F32)
        tail_spec = row_spec
    out_shape = [bf, kv, kv, kvb, kvb, bf, bf, tail_shape]
    out_specs = [row_spec, kv_spec, kv_spec, kvb_spec, kvb_spec, row_spec, row_spec, tail_spec]
    return pl.pallas_call(
        functools.partial(_proj_kernel, d=d, tm=tm, seg=seg, scale=scale),
        grid=grid, in_specs=in_specs, out_specs=out_specs, out_shape=out_shape,
        scratch_shapes=[pltpu.VMEM((tm, d), _BF16),
                        pltpu.VMEM((tm + CONV_TAIL, d), _F32),
                        pltpu.VMEM((tm, d), _F32)],
        compiler_params=pltpu.CompilerParams(
            dimension_semantics=("arbitrary", "arbitrary"), vmem_limit_bytes=VMEM_LIMIT_BYTES),
        name="proj_carried" if carried else "proj_segmented",
    )(*args)


def _attend(tiles, kblk, vblk, keys_on_lanes, tt_ref, carry_ref, oacc_ref, s_ref, hl_ref):
    n = len(tiles)
    d = tiles[0][0].shape[1]
    starts = [sum(t[3] for t in tiles[:r]) for r in range(n + 1)]
    m = starts[n]
    n_slabs = d // LANES
    half = LANES // HEADS_PER_SLAB

    def _rows(x, r):
        return x[starts[r]:starts[r + 1]]

    def block_diagonal(blk, p):
        if keys_on_lanes:
            x = blk[p * LANES:(p + 1) * LANES, :]
            zero = jnp.zeros((half, KEY_BLOCK), x.dtype)
            return jnp.concatenate([jnp.concatenate([x[:half], zero], axis=1),
                                    jnp.concatenate([zero, x[half:]], axis=1)], axis=0)
        x = blk[:, p * LANES:(p + 1) * LANES]
        first_head = lax.broadcasted_iota(jnp.int32, x.shape, 1) < half
        return jnp.concatenate([jnp.where(first_head, x, jnp.zeros_like(x)),
                                jnp.where(first_head, jnp.zeros_like(x), x)], axis=0)

    def masked(x):
        pieces = [_rows(x, r) if t[2] is None else jnp.where(t[2], _rows(x, r), 0.0)
                  for r, t in enumerate(tiles)]
        return jnp.concatenate(pieces, axis=0) if n > 1 else pieces[0]

    def scores_and_softplus(p):
        sl = slice(p * LANES, (p + 1) * LANES)
        kbd = block_diagonal(kblk, p)
        q_rows = [t[0][0:t[3], sl] for t in tiles]
        q2 = jnp.concatenate(q_rows, axis=0) if n > 1 else q_rows[0]
        if keys_on_lanes:
            s = jnp.dot(q2, kbd, preferred_element_type=_F32)
        else:
            s = lax.dot_general(q2, kbd, _NT, preferred_element_type=_F32)
        s_ref[p, 0:m, :] = s
        soft = masked(jnp.maximum(s, 0.0) + jnp.log(1.0 + jnp.exp2(jnp.abs(s) * -LOG2_E)))
        hi = soft.astype(_BF16)
        lo = (soft - hi.astype(_F32)).astype(_BF16)
        for h in range(HEADS_PER_SLAB):
            hs = slice(h * KEY_BLOCK, (h + 1) * KEY_BLOCK)
            r0 = (HEADS_PER_SLAB * p + h) * m
            hl_ref[r0:r0 + m, :] = jnp.concatenate([hi[:, hs], lo[:, hs]], axis=1)

    def weights_and_values(p, cr):
        sl = slice(p * LANES, (p + 1) * LANES)
        vbd = block_diagonal(vblk, p)
        per_head = [cr[h * m:(h + 1) * m] for h in range(HEADS_PER_SLAB)]
        csum = jnp.concatenate([c[:, :KEY_BLOCK] for c in per_head], axis=1)
        rsum = jnp.concatenate([c[:, KEY_BLOCK:] for c in per_head], axis=1)
        carries = [carry_ref[t[1], p, 0:t[3]] for t in tiles]
        carry = jnp.concatenate(carries, axis=0) if n > 1 else carries[0]
        a = masked(jnp.exp(s_ref[p, 0:m, :] - csum - carry)).astype(_BF16)
        if keys_on_lanes:
            o = lax.dot_general(a, vbd, _NT, preferred_element_type=_F32)
        else:
            o = jnp.dot(a, vbd, preferred_element_type=_F32)
        carry = carry + rsum
        for r, (_, slot, _, rows) in enumerate(tiles):
            oacc_ref[slot, 0:rows, sl] += _rows(o, r)
            carry_ref[slot, p, 0:rows] = _rows(carry, r)
        late = _rows(carry, n - 2)[LATE_ROWS:] if n > 1 else None
        return _rows(carry, n - 1), late

    for p in range(n_slabs):
        scores_and_softplus(p)
    rows_per_slab = HEADS_PER_SLAB * m
    cr = jnp.dot(hl_ref[0:n_slabs * rows_per_slab, :], tt_ref[...], preferred_element_type=_F32)
    smallest = smallest_late = None
    for p in range(n_slabs):
        last, late = weights_and_values(p, cr[p * rows_per_slab:(p + 1) * rows_per_slab])
        smallest = last if smallest is None else jnp.minimum(smallest, last)
        if late is not None:
            smallest_late = late if smallest_late is None else jnp.minimum(smallest_late, late)
    return jnp.min(smallest), None if smallest_late is None else jnp.min(smallest_late)


def _causal_mask(tq):
    row = lax.broadcasted_iota(jnp.int32, (tq, KEY_BLOCK), 0)
    col = lax.broadcasted_iota(jnp.int32, (tq, KEY_BLOCK), 1)
    m = col < row
    return jnp.concatenate([m] * HEADS_PER_SLAB, axis=1)


def _walk_back(first_block, smallest, load_block, tile, tt_ref, carry_ref, oacc_ref, s_ref, hl_ref):
    def cond(state):
        kb, m = state
        return jnp.logical_and(kb >= 0, m <= CARRY_DONE)

    def body(state):
        kb, _ = state
        kblk, vblk = load_block(kb)
        m, _ = _attend([tile], kblk, vblk, True, tt_ref, carry_ref, oacc_ref, s_ref, hl_ref)
        return kb - 1, m

    lax.while_loop(cond, body, (jnp.asarray(first_block, jnp.int32), smallest))


def _attn_prompt_kernel(q0_ref, q1_ref, q2_ref, k_ref, v_ref, tt_ref, o_ref,
                        carry_ref, oacc_ref, s_ref, hl_ref, m_ref, *, n_blocks):
    g = pl.program_id(1)
    j = n_blocks - 1 - g
    tq, d = q0_ref.shape
    slots = [lax.rem(j + TILES_IN_FLIGHT + r, TILES_IN_FLIGHT) for r in range(TILES_IN_FLIGHT)]

    def load_block(kb):
        start = pl.multiple_of(kb * KEY_BLOCK, KEY_BLOCK)
        return k_ref[:, pl.ds(start, KEY_BLOCK)], v_ref[:, pl.ds(start, KEY_BLOCK)]

    @pl.when(g == 0)
    def _():
        m_ref[1] = 0.0

    late_rows_done = m_ref[1] > CARRY_DONE

    def batch(last_tile_rows):
        kblk, vblk = load_block(j)
        tiles = [(q0_ref, slots[0], _causal_mask(tq), tq), (q1_ref, slots[1], None, tq),
                 (q2_ref, slots[2], None, last_tile_rows)]
        m_ref[0], m_ref[1] = _attend(tiles, kblk, vblk, True, tt_ref, carry_ref, oacc_ref, s_ref, hl_ref)

    @pl.when(j >= 0)
    def _():
        carry_ref[slots[0]] = jnp.zeros(carry_ref.shape[1:], _F32)
        oacc_ref[slots[0]] = jnp.zeros(oacc_ref.shape[1:], _F32)
        pl.when(late_rows_done)(lambda: batch(LATE_ROWS))
        pl.when(jnp.logical_not(late_rows_done))(lambda: batch(tq))

    first_block = jnp.where(j + TILES_IN_FLIGHT - 1 < n_blocks, j - 1, -1)
    _walk_back(first_block, m_ref[0], load_block, (q2_ref, slots[2], None, tq),
               tt_ref, carry_ref, oacc_ref, s_ref, hl_ref)
    o_ref[...] = oacc_ref[slots[2]].astype(_BF16)


def _attn_sample_kernel(q_ref, kn_ref, vn_ref, kc_ref, vc_ref, tt_ref, o_ref,
                        carry_ref, oacc_ref, s_ref, hl_ref):
    tq, d = q_ref.shape
    past = kc_ref.shape[1]
    carry_ref[...] = jnp.zeros(carry_ref.shape, _F32)
    oacc_ref[...] = jnp.zeros(oacc_ref.shape, _F32)
    pad = jnp.zeros((KEY_BLOCK - tq, d), _BF16)
    kblk = jnp.concatenate([kn_ref[...], pad], axis=0)
    vblk = jnp.concatenate([vn_ref[...], pad], axis=0)
    m, _ = _attend([(q_ref, 0, _causal_mask(tq), tq)], kblk, vblk, False,
                   tt_ref, carry_ref, oacc_ref, s_ref, hl_ref)

    def load_block(kb):
        start = pl.multiple_of(kb * KEY_BLOCK, KEY_BLOCK)
        return (kc_ref[:, pl.ds(start, KEY_BLOCK)].astype(_BF16),
                vc_ref[:, pl.ds(start, KEY_BLOCK)].astype(_BF16))

    _walk_back(past // KEY_BLOCK - 1, m, load_block, (q_ref, 0, None, tq),
               tt_ref, carry_ref, oacc_ref, s_ref, hl_ref)
    o_ref[...] = oacc_ref[0].astype(_BF16)


def _cumsum_matrix():
    j = lax.broadcasted_iota(jnp.int32, (KEY_BLOCK, KEY_BLOCK), 0)
    s = lax.broadcasted_iota(jnp.int32, (KEY_BLOCK, KEY_BLOCK), 1)
    half = jnp.concatenate([(j >= s).astype(_BF16), jnp.ones((KEY_BLOCK, KEY_BLOCK), _BF16)], axis=1)
    return jnp.concatenate([half, half], axis=0)


def _attn_scratch(n_tiles, tq, d):
    n_slabs = d // LANES
    width = HEADS_PER_SLAB * KEY_BLOCK
    return [pltpu.VMEM((n_tiles, n_slabs, tq, width), _F32),
            pltpu.VMEM((n_tiles, tq, d), _F32),
            pltpu.VMEM((n_slabs, n_tiles * tq, width), _F32),
            pltpu.VMEM((HEADS_PER_SLAB * n_slabs * n_tiles * tq, width), _BF16)]


def _attention_prompt(q, kt, vt):
    bsz, d, seq = kt.shape
    tq = KEY_BLOCK
    n_blocks = seq // tq
    tt = _cumsum_matrix()
    seq_spec = pl.BlockSpec((None, d, seq), lambda b, g: (b, 0, 0))

    def q_spec(r):
        return pl.BlockSpec((tq, d), lambda b, g: (b * n_blocks + jnp.clip(n_blocks - 1 - g + r, 0, n_blocks - 1), 0))

    return pl.pallas_call(
        functools.partial(_attn_prompt_kernel, n_blocks=n_blocks),
        grid=(bsz, n_blocks + TILES_IN_FLIGHT - 1),
        in_specs=[q_spec(0), q_spec(1), q_spec(2), seq_spec, seq_spec,
                  pl.BlockSpec(tt.shape, lambda b, g: (0, 0))],
        out_specs=q_spec(TILES_IN_FLIGHT - 1),
        out_shape=jax.ShapeDtypeStruct(q.shape, _BF16),
        scratch_shapes=_attn_scratch(TILES_IN_FLIGHT, tq, d) + [pltpu.SMEM((2,), _F32)],
        compiler_params=pltpu.CompilerParams(
            dimension_semantics=("arbitrary", "arbitrary"), vmem_limit_bytes=VMEM_LIMIT_BYTES),
        name="attn_prompt",
    )(q, q, q, kt, vt, tt)


def _attention_sample(q, kb_new, vb_new, cache_kt, cache_vt, tq):
    bsz, d, past = cache_kt.shape
    tt = _cumsum_matrix()
    new_spec = pl.BlockSpec((tq, d), lambda b: (b, 0))
    cache_spec = pl.BlockSpec((None, d, past), lambda b: (b, 0, 0))
    return pl.pallas_call(
        _attn_sample_kernel,
        grid=(bsz,),
        in_specs=[new_spec, new_spec, new_spec, cache_spec, cache_spec,
                  pl.BlockSpec(tt.shape, lambda b: (0, 0))],
        out_specs=new_spec,
        out_shape=jax.ShapeDtypeStruct(q.shape, _BF16),
        scratch_shapes=_attn_scratch(1, tq, d),
        compiler_params=pltpu.CompilerParams(
            dimension_semantics=("arbitrary",), vmem_limit_bytes=VMEM_LIMIT_BYTES),
        name="attn_sample",
    )(q, kb_new, vb_new, cache_kt, cache_vt, tt)


def _ffn_kernel(x_ref, mc_ref, ga_ref, at_ref, wo_ref, gf_ref, wu_ref, wd_ref, gl_ref, y_ref,
                h_ref, hn_ref, *, f_chunk, final_norm):
    mixed = mc_ref[...] + ga_ref[...] * at_ref[...]
    h = x_ref[...] + jnp.dot(mixed, wo_ref[...], preferred_element_type=_F32)
    h_ref[...] = h
    hn_ref[...] = (h * _rms_scale(h) * gf_ref[...]).astype(_BF16)
    d_ff = wu_ref.shape[1]
    acc = h_ref[...]
    for c in range(d_ff // f_chunk):
        cs = slice(c * f_chunk, (c + 1) * f_chunk)
        up = jnp.dot(hn_ref[...], wu_ref[:, cs], preferred_element_type=_F32)
        act = jnp.square(jnp.maximum(up, 0.0)).astype(_BF16)
        acc = acc + jnp.dot(act, wd_ref[cs, :], preferred_element_type=_F32)
    y_ref[...] = acc * _rms_scale(acc) * gl_ref[...] if final_norm else acc


def _merge_ffn(x2, mc, ga, attn, wo_bf, g_ffn, wu_bf, wd_bf, g_final, *, tm, final_norm):
    rows, d = x2.shape
    d_ff = wu_bf.shape[1]
    row_spec = pl.BlockSpec((tm, d), lambda i: (i, 0))
    const = lambda i: (0, 0)
    once = pl.Buffered(1)
    return pl.pallas_call(
        functools.partial(_ffn_kernel, f_chunk=d, final_norm=final_norm),
        grid=(rows // tm,),
        in_specs=[row_spec, row_spec, row_spec, row_spec,
                  pl.BlockSpec((d, d), const, pipeline_mode=once),
                  pl.BlockSpec((1, d), const),
                  pl.BlockSpec((d, d_ff), const, pipeline_mode=once),
                  pl.BlockSpec((d_ff, d), const, pipeline_mode=once),
                  pl.BlockSpec((1, d), const)],
        out_specs=row_spec,
        out_shape=jax.ShapeDtypeStruct((rows, d), _F32),
        scratch_shapes=[pltpu.VMEM((tm, d), _F32), pltpu.VMEM((tm, d), _BF16)],
        compiler_params=pltpu.CompilerParams(
            dimension_semantics=("arbitrary",), vmem_limit_bytes=VMEM_LIMIT_BYTES),
        name="merge_ffn",
    )(x2, mc, ga, attn, wo_bf, g_ffn.reshape(1, d), wu_bf, wd_bf, g_final.reshape(1, d))


def _row_tile(rows, limit=512):
    tm = min(rows, limit)
    assert rows % tm == 0 and tm % CONV_TAIL == 0
    return tm


def kernel(x_prompt, x_sample, cache_conv, cache_k, cache_v, g_mix, w_in, conv_w, w_out,
           g_ffn, w_up, w_down, g_final):
    depth = w_in.shape[0]
    bsz, seq, d = x_prompt.shape
    dbsz, dseq, _ = x_sample.shape
    past, n_heads, head_dim = cache_k.shape[2], cache_k.shape[3], cache_k.shape[4]
    assert n_heads * head_dim == d and head_dim * HEADS_PER_SLAB == LANES
    assert seq % KEY_BLOCK == 0 and seq // KEY_BLOCK >= TILES_IN_FLIGHT and past % KEY_BLOCK == 0
    assert CONV_WIDTH - 1 <= dseq <= KEY_BLOCK

    def feature_major(c):
        return jnp.transpose(c, (0, 2, 3, 1)).reshape(c.shape[0], d, c.shape[1])

    def position_major(ct):
        return jnp.transpose(ct, (0, 3, 1, 2))

    xp, xs = x_prompt, x_sample
    outs = {name: [] for name in ("conv_p", "k_p", "v_p", "conv_s", "k_s", "v_s")}
    for l in range(depth):
        w_bf = w_in[l].astype(_BF16)
        k0, k1 = 4 * d, 6 * d
        w_rest = jnp.concatenate([w_in[l][:, :k0], w_in[l][:, k1:]], axis=1).astype(_BF16)
        wkv_t = w_in[l][:, k0:k1].T.astype(_BF16)
        wo_bf = w_out[l].astype(_BF16)
        wu_bf = w_up[l].astype(_BF16)
        wd_bf = w_down[l].astype(_BF16)

        tm = _row_tile(seq)
        q, kt, vt, ktb, vtb, mc, ga, tail = _project(xp, g_mix[l], w_rest, conv_w[l], tm=tm, seg=tm,
                                                     n_heads=n_heads, wkv_t=wkv_t)
        attn = _attention_prompt(q, ktb, vtb)
        outs["conv_p"].append(tail[:, CONV_TAIL - (CONV_WIDTH - 1):, :])
        outs["k_p"].append(position_major(kt))
        outs["v_p"].append(position_major(vt))
        xp = _merge_ffn(xp.reshape(bsz * seq, d), mc, ga, attn, wo_bf, g_ffn[l], wu_bf, wd_bf,
                        g_final, tm=tm, final_norm=l == depth - 1).reshape(bsz, seq, d)

        rows = dbsz * dseq
        hist = cache_conv[l].astype(_F32)
        prev1 = jnp.zeros((dbsz, dseq, d), _F32).at[:, 0].set(hist[:, 1]).reshape(rows, d)
        prev2 = (jnp.zeros((dbsz, dseq, d), _F32).at[:, 0].set(hist[:, 0]).at[:, 1].set(hist[:, 1])
                 .reshape(rows, d))
        q, k, v, kb, vb, mc, ga, u = _project(xs, g_mix[l], w_bf, conv_w[l], tm=rows, seg=dseq,
                                             n_heads=n_heads, prev=(prev1, prev2))
        attn = _attention_sample(q, kb, vb, feature_major(cache_k[l]), feature_major(cache_v[l]), dseq)
        outs["conv_s"].append(u.reshape(dbsz, dseq, d)[:, dseq - (CONV_WIDTH - 1):, :])
        outs["k_s"].append(k.reshape(dbsz, dseq, n_heads, head_dim))
        outs["v_s"].append(v.reshape(dbsz, dseq, n_heads, head_dim))
        xs = _merge_ffn(xs.reshape(rows, d), mc, ga, attn, wo_bf, g_ffn[l], wu_bf, wd_bf,
                        g_final, tm=_row_tile(rows), final_norm=l == depth - 1).reshape(dbsz, dseq, d)

    return (xp, xs, jnp.stack(outs["conv_p"]), jnp.stack(outs["k_p"]), jnp.stack(outs["v_p"]),
            jnp.stack(outs["conv_s"]), jnp.stack(outs["k_s"]), jnp.stack(outs["v_s"]))
```

```python
import functools

import jax
import jax.numpy as jnp
from jax import lax
from jax.experimental import pallas as pl
from jax.experimental.pallas import tpu as pltpu

RMS_EPS = 1e-6
CONV_WIDTH = 3
LANES = 128
KEY_BLOCK = 128
HEADS_PER_SLAB = 2
TILES_IN_FLIGHT = 3
FFN_ROWS = 1024
LATE_ROWS = 32
CONV_TAIL = 8
VMEM_LIMIT_BYTES = 60 * 1024 * 1024
CARRY_DONE = 104.0
LOG2_E = 1.4426950408889634
MASKED_SCORE = -1e30

_F32 = jnp.float32
_BF16 = jnp.bfloat16
_NT = (((1,), (1,)), ((), ()))


def _rms_scale(x32):
    return lax.rsqrt(jnp.mean(x32 * x32, axis=-1, keepdims=True) + RMS_EPS)


def _sigmoid(z):
    return 1.0 / (1.0 + jnp.exp(-z))


def _proj_kernel(*refs, d, tm, seg, scale):
    carried = seg == tm
    if carried:
        (x_ref, g_ref, w_ref, wkv_ref, cw_ref,
         q_ref, k_ref, v_ref, kb_ref, vb_ref, mc_ref, ga_ref, tail_ref,
         xn_ref, u_ref, conv_ref) = refs
        b_g, c_g, h_g, q_g, gc_g, ga_g = range(6)
    else:
        (x_ref, g_ref, w_ref, cw_ref, p1_ref, p2_ref,
         q_ref, k_ref, v_ref, kb_ref, vb_ref, mc_ref, ga_ref, tail_ref,
         xn_ref, u_ref, conv_ref) = refs
        b_g, c_g, h_g, q_g, k_g, v_g, gc_g, ga_g = range(8)

    if carried:
        @pl.when(pl.program_id(1) == 0)
        def _():
            u_ref[0:CONV_TAIL, :] = jnp.zeros((CONV_TAIL, d), _F32)

    x32 = x_ref[...]
    xn_ref[...] = (x32 * _rms_scale(x32) * g_ref[...]).astype(_BF16)

    def proj(group):
        return jnp.dot(xn_ref[...], w_ref[:, group * d:(group + 1) * d], preferred_element_type=_F32)

    u = proj(c_g) * proj(h_g)
    cw = cw_ref[...]
    if carried:
        u_ref[CONV_TAIL:CONV_TAIL + tm, :] = u
        s1 = u_ref[CONV_TAIL - 1:CONV_TAIL - 1 + tm, :]
        s2 = u_ref[CONV_TAIL - 2:CONV_TAIL - 2 + tm, :]
        conv_ref[...] = cw[0:1, :] * s2 + cw[1:2, :] * s1 + cw[2:3, :] * u
        tail = u[tm - CONV_TAIL:tm, :]
        u_ref[0:CONV_TAIL, :] = tail
        tail_ref[...] = tail
    else:
        row = lax.broadcasted_iota(jnp.int32, (tm, d), 0) % seg
        s1 = jnp.where(row == 0, p1_ref[...], pltpu.roll(u, 1, 0))
        s2 = jnp.where(row < 2, p2_ref[...], pltpu.roll(u, 2, 0))
        conv_ref[...] = cw[0:1, :] * s2 + cw[1:2, :] * s1 + cw[2:3, :] * u
        tail_ref[...] = u

    conv_out = proj(b_g) * conv_ref[...]
    mc_ref[...] = (_sigmoid(proj(gc_g)) * conv_out).astype(_BF16)
    ga_ref[...] = _sigmoid(proj(ga_g)).astype(_BF16)
    q_ref[...] = (proj(q_g) * scale).astype(_BF16)
    if carried:
        for i, (o_ref, ob_ref) in enumerate(((k_ref, kb_ref), (v_ref, vb_ref))):
            t = lax.dot_general(wkv_ref[i * d:(i + 1) * d, :], xn_ref[...], _NT,
                                preferred_element_type=_F32)
            o_ref[...] = t.reshape(o_ref.shape)
            ob_ref[...] = t.astype(_BF16)
    else:
        for group, o_ref, ob_ref in ((k_g, k_ref, kb_ref), (v_g, v_ref, vb_ref)):
            t = proj(group)
            o_ref[...] = t.reshape(o_ref.shape)
            ob_ref[...] = t.astype(_BF16)


def _project(x, g, w_bf, conv_w, *, tm, seg, n_heads, wkv_t=None, prev=None):
    bsz, t, d = x.shape
    carried = prev is None
    n_t = t // tm if carried else 1
    rows = bsz * t
    x2 = x.reshape(rows, d)
    head_dim = d // n_heads
    scale = float(head_dim) ** -0.5

    if carried:
        grid = (bsz, n_t)
        row_map = lambda b, j: (b * n_t + j, 0)
    else:
        assert rows == tm
        grid = (1, 1)
        row_map = lambda b, j: (0, 0)
    const = lambda b, j: (0, 0)
    once = pl.Buffered(1)
    row_spec = pl.BlockSpec((tm, d), row_map)
    in_specs = [row_spec, pl.BlockSpec((1, d), const), pl.BlockSpec(w_bf.shape, const, pipeline_mode=once)]
    args = [x2, g.reshape(1, d), w_bf]
    if carried:
        in_specs.append(pl.BlockSpec(wkv_t.shape, const, pipeline_mode=once))
        args.append(wkv_t)
    in_specs.append(pl.BlockSpec((CONV_WIDTH, d), const))
    args.append(conv_w)
    bf = jax.ShapeDtypeStruct((rows, d), _BF16)
    if carried:
        kv = jax.ShapeDtypeStruct((bsz, n_heads, head_dim, t), _F32)
        kv_spec = pl.BlockSpec((None, n_heads, head_dim, tm), lambda b, j: (b, 0, 0, j))
        kvb = jax.ShapeDtypeStruct((bsz, d, t), _BF16)
        kvb_spec = pl.BlockSpec((None, d, tm), lambda b, j: (b, 0, j))
        tail_shape = jax.ShapeDtypeStruct((bsz, CONV_TAIL, d), _F32)
        tail_spec = pl.BlockSpec((None, CONV_TAIL, d), lambda b, j: (b, 0, 0))
    else:
        in_specs += [row_spec, row_spec]
        args += [prev[0], prev[1]]
        kv = jax.ShapeDtypeStruct((rows, n_heads, head_dim), _F32)
        kv_spec = pl.BlockSpec((tm, n_heads, head_dim), lambda b, j: (0, 0, 0))
        kvb, kvb_spec = bf, row_spec
        tail_shape = jax.ShapeDtypeStruct((rows, d), _F32)
        tail_spec = row_spec
    out_shape = [bf, kv, kv, kvb, kvb, bf, bf, tail_shape]
    out_specs = [row_spec, kv_spec, kv_spec, kvb_spec, kvb_spec, row_spec, row_spec, tail_spec]
    return pl.pallas_call(
        functools.partial(_proj_kernel, d=d, tm=tm, seg=seg, scale=scale),
        grid=grid, in_specs=in_specs, out_specs=out_specs, out_shape=out_shape,
        scratch_shapes=[pltpu.VMEM((tm, d), _BF16),
                        pltpu.VMEM((tm + CONV_TAIL, d), _F32),
                        pltpu.VMEM((tm, d), _F32)],
        compiler_params=pltpu.CompilerParams(
            dimension_semantics=("arbitrary", "arbitrary"), vmem_limit_bytes=VMEM_LIMIT_BYTES),
        name="proj_carried" if carried else "proj_segmented",
    )(*args)


def _stages(tiles, d, keys_on_lanes):
    n = len(tiles)
    starts = [sum(t[4] for t in tiles[:r]) for r in range(n + 1)]
    m = starts[n]
    n_slabs = d // LANES
    half = LANES // HEADS_PER_SLAB
    rows_per_slab = HEADS_PER_SLAB * m

    def _rows(x, r):
        return x[starts[r]:starts[r + 1]]

    def block_diagonal(blk, p):
        if keys_on_lanes:
            x = blk[p * LANES:(p + 1) * LANES, :]
            zero = jnp.zeros((half, KEY_BLOCK), x.dtype)
            return jnp.concatenate([jnp.concatenate([x[:half], zero], axis=1),
                                    jnp.concatenate([zero, x[half:]], axis=1)], axis=0)
        x = blk[:, p * LANES:(p + 1) * LANES]
        first_head = lax.broadcasted_iota(jnp.int32, x.shape, 1) < half
        return jnp.concatenate([jnp.where(first_head, x, jnp.zeros_like(x)),
                                jnp.where(first_head, jnp.zeros_like(x), x)], axis=0)

    def masked(s):
        pieces = [_rows(s, r) if t[2] is None else jnp.where(t[2], _rows(s, r), MASKED_SCORE)
                  for r, t in enumerate(tiles)]
        return jnp.concatenate(pieces, axis=0) if n > 1 else pieces[0]

    def scores_and_softplus(p, kblk, s_ref, hl_ref):
        sl = slice(p * LANES, (p + 1) * LANES)
        kbd = block_diagonal(kblk, p)
        q_rows = [t[0][t[3]:t[3] + t[4], sl] for t in tiles]
        q2 = jnp.concatenate(q_rows, axis=0) if n > 1 else q_rows[0]
        if keys_on_lanes:
            s = jnp.dot(q2, kbd, preferred_element_type=_F32)
        else:
            s = lax.dot_general(q2, kbd, _NT, preferred_element_type=_F32)
        s = masked(s)
        s_ref[p, 0:m, :] = s
        soft = jnp.maximum(s, 0.0) + jnp.log(1.0 + jnp.exp2(jnp.abs(s) * -LOG2_E))
        hi = soft.astype(_BF16)
        lo = (soft - hi.astype(_F32)).astype(_BF16)
        for h in range(HEADS_PER_SLAB):
            hs = slice(h * KEY_BLOCK, (h + 1) * KEY_BLOCK)
            r0 = (HEADS_PER_SLAB * p + h) * m
            hl_ref[r0:r0 + m, :] = jnp.concatenate([hi[:, hs], lo[:, hs]], axis=1)

    def weights_and_values(p, vblk, cr, carry_ref, oacc_ref, s_ref):
        sl = slice(p * LANES, (p + 1) * LANES)
        vbd = block_diagonal(vblk, p)
        per_head = [cr[h * m:(h + 1) * m] for h in range(HEADS_PER_SLAB)]
        csum = jnp.concatenate([c[:, :KEY_BLOCK] for c in per_head], axis=1)
        rsum = jnp.concatenate([c[:, KEY_BLOCK:] for c in per_head], axis=1)
        carries = [carry_ref[t[1], p, t[3]:t[3] + t[4]] for t in tiles]
        carry = jnp.concatenate(carries, axis=0) if n > 1 else carries[0]
        a = jnp.exp(s_ref[p, 0:m, :] - csum - carry).astype(_BF16)
        if keys_on_lanes:
            o = lax.dot_general(a, vbd, _NT, preferred_element_type=_F32)
        else:
            o = jnp.dot(a, vbd, preferred_element_type=_F32)
        carry = carry + rsum
        for r, (_, slot, _, row0, rows) in enumerate(tiles):
            oacc_ref[slot, row0:row0 + rows, sl] += _rows(o, r)
            carry_ref[slot, p, row0:row0 + rows] = _rows(carry, r)
        late = _rows(carry, n - 2)[LATE_ROWS:] if n > 1 else None
        return _rows(carry, n - 1), late

    def scores(kblk, s_ref, hl_ref):
        for p in range(n_slabs):
            scores_and_softplus(p, kblk, s_ref, hl_ref)

    def cumsum(hl_ref, tt_ref):
        return jnp.dot(hl_ref[0:n_slabs * rows_per_slab, :], tt_ref[...], preferred_element_type=_F32)

    def values(vblk, cr, carry_ref, oacc_ref, s_ref, before_slab=None):
        smallest = smallest_late = None
        for p in range(n_slabs):
            if before_slab is not None:
                before_slab(p)
            last, late = weights_and_values(p, vblk, cr[p * rows_per_slab:(p + 1) * rows_per_slab],
                                            carry_ref, oacc_ref, s_ref)
            smallest = last if smallest is None else jnp.minimum(smallest, last)
            if late is not None:
                smallest_late = late if smallest_late is None else jnp.minimum(smallest_late, late)
        return jnp.min(smallest), None if smallest_late is None else jnp.min(smallest_late)

    return scores, cumsum, values, scores_and_softplus


def _attend(tiles, kblk, vblk, keys_on_lanes, tt_ref, carry_ref, oacc_ref, s_ref, hl_ref):
    scores, cumsum, values, _ = _stages(tiles, kblk.shape[0 if keys_on_lanes else 1], keys_on_lanes)
    scores(kblk, s_ref, hl_ref)
    return values(vblk, cumsum(hl_ref, tt_ref), carry_ref, oacc_ref, s_ref)


def _causal_mask(tq):
    row = lax.broadcasted_iota(jnp.int32, (tq, KEY_BLOCK), 0)
    col = lax.broadcasted_iota(jnp.int32, (tq, KEY_BLOCK), 1)
    m = col < row
    return jnp.concatenate([m] * HEADS_PER_SLAB, axis=1)


def _walk_back(first_block, smallest, load_block, tile, tt_ref, carry_ref, oacc_ref, s_ref, hl_ref):
    def cond(state):
        kb, m = state
        return jnp.logical_and(kb >= 0, m <= CARRY_DONE)

    def body(state):
        kb, _ = state
        kblk, vblk = load_block(kb)
        m, _ = _attend([tile], kblk, vblk, True, tt_ref, carry_ref, oacc_ref, s_ref, hl_ref)
        return kb - 1, m

    lax.while_loop(cond, body, (jnp.asarray(first_block, jnp.int32), smallest))


def _attn_prompt_kernel(q0_ref, q1_ref, q2_ref, q3_ref, k_ref, v_ref, tt_ref, o_ref,
                        carry_ref, oacc_ref, sa_ref, hla_ref, sb_ref, hlb_ref, s1_ref, hl1_ref, m_ref,
                        *, n_blocks):
    g = pl.program_id(1)
    ja = n_blocks - 1 - g
    jb = ja + 1
    tq, d = q0_ref.shape
    slots = [lax.rem(jb + TILES_IN_FLIGHT + r, TILES_IN_FLIGHT) for r in range(TILES_IN_FLIGHT)]
    generic = (tt_ref, carry_ref, oacc_ref, s1_ref, hl1_ref)

    def load_block(kb):
        start = pl.multiple_of(jnp.clip(kb, 0, n_blocks - 1) * KEY_BLOCK, KEY_BLOCK)
        return k_ref[:, pl.ds(start, KEY_BLOCK)], v_ref[:, pl.ds(start, KEY_BLOCK)]

    @pl.when(g == 0)
    def _():
        carry_ref[...] = jnp.zeros(carry_ref.shape, _F32)
        oacc_ref[...] = jnp.zeros(oacc_ref.shape, _F32)
        sb_ref[...] = jnp.zeros(sb_ref.shape, _F32)
        hlb_ref[...] = jnp.zeros(hlb_ref.shape, _BF16)
        m_ref[0] = 0.0
        m_ref[1] = 0.0

    late_rows_done = m_ref[1] > CARRY_DONE
    last_tile_live = jb + TILES_IN_FLIGHT - 1 < n_blocks

    def step(s_new, hl_new, s_old, hl_old):
        mask = _causal_mask(tq)
        rows = ((0, tq), (0, tq), (0, LATE_ROWS))
        q_refs = (q0_ref, q1_ref, q2_ref)
        tiles_a = [(q_refs[r], None, mask if r == 0 else None) + rows[r] for r in range(TILES_IN_FLIGHT)]
        tiles_b = [(None, slots[r], mask if r == 0 else None) + rows[r] for r in range(TILES_IN_FLIGHT)]
        _, _, _, scores_slab_a = _stages(tiles_a, d, True)
        _, cumsum_b, values_b, _ = _stages(tiles_b, d, True)
        carry_ref[slots[0]] = jnp.zeros(carry_ref.shape[1:], _F32)
        oacc_ref[slots[0]] = jnp.zeros(oacc_ref.shape[1:], _F32)
        kblk = load_block(ja)[0]
        cr = cumsum_b(hl_old, tt_ref)
        m_ref[0], m_ref[1] = values_b(load_block(jb)[1], cr, carry_ref, oacc_ref, s_old,
                                      before_slab=lambda p: scores_slab_a(p, kblk, s_new, hl_new))

    @pl.when(g <= n_blocks)
    def _():
        even = lax.rem(g, 2) == 0
        pl.when(even)(lambda: step(sa_ref, hla_ref, sb_ref, hlb_ref))
        pl.when(jnp.logical_not(even))(lambda: step(sb_ref, hlb_ref, sa_ref, hla_ref))

    @pl.when(jnp.logical_and(jnp.logical_and(last_tile_live, jb >= 0), jnp.logical_not(late_rows_done)))
    def _():
        late_tile = (q3_ref, slots[2], None, LATE_ROWS, tq - LATE_ROWS)
        m_late, _ = _attend([late_tile], *load_block(jb), True, *generic)
        m_ref[0] = jnp.minimum(m_ref[0], m_late)

    first_block = jnp.where(last_tile_live, jb - 1, -1)
    _walk_back(first_block, m_ref[0], load_block, (q3_ref, slots[2], None, 0, tq), *generic)
    o_ref[...] = oacc_ref[slots[2]].astype(_BF16)


def _attn_sample_kernel(q_ref, kn_ref, vn_ref, kc_ref, vc_ref, tt_ref, o_ref,
                        carry_ref, oacc_ref, s_ref, hl_ref):
    tq, d = q_ref.shape
    past = kc_ref.shape[1]
    carry_ref[...] = jnp.zeros(carry_ref.shape, _F32)
    oacc_ref[...] = jnp.zeros(oacc_ref.shape, _F32)
    pad = jnp.zeros((KEY_BLOCK - tq, d), _BF16)
    kblk = jnp.concatenate([kn_ref[...], pad], axis=0)
    vblk = jnp.concatenate([vn_ref[...], pad], axis=0)
    m, _ = _attend([(q_ref, 0, _causal_mask(tq), 0, tq)], kblk, vblk, False,
                   tt_ref, carry_ref, oacc_ref, s_ref, hl_ref)

    def load_block(kb):
        start = pl.multiple_of(kb * KEY_BLOCK, KEY_BLOCK)
        return (kc_ref[:, pl.ds(start, KEY_BLOCK)].astype(_BF16),
                vc_ref[:, pl.ds(start, KEY_BLOCK)].astype(_BF16))

    _walk_back(past // KEY_BLOCK - 1, m, load_block, (q_ref, 0, None, 0, tq),
               tt_ref, carry_ref, oacc_ref, s_ref, hl_ref)
    o_ref[...] = oacc_ref[0].astype(_BF16)


def _cumsum_matrix():
    j = lax.broadcasted_iota(jnp.int32, (KEY_BLOCK, KEY_BLOCK), 0)
    s = lax.broadcasted_iota(jnp.int32, (KEY_BLOCK, KEY_BLOCK), 1)
    half = jnp.concatenate([(j >= s).astype(_BF16), jnp.ones((KEY_BLOCK, KEY_BLOCK), _BF16)], axis=1)
    return jnp.concatenate([half, half], axis=0)


def _state_scratch(n_tiles, tq, d):
    width = HEADS_PER_SLAB * KEY_BLOCK
    return [pltpu.VMEM((n_tiles, d // LANES, tq, width), _F32),
            pltpu.VMEM((n_tiles, tq, d), _F32)]


def _block_scratch(rows, d):
    n_slabs = d // LANES
    width = HEADS_PER_SLAB * KEY_BLOCK
    return [pltpu.VMEM((n_slabs, rows, width), _F32),
            pltpu.VMEM((HEADS_PER_SLAB * n_slabs * rows, width), _BF16)]


def _attention_prompt(q, kt, vt):
    bsz, d, seq = kt.shape
    tq = KEY_BLOCK
    n_blocks = seq // tq
    tt = _cumsum_matrix()
    seq_spec = pl.BlockSpec((None, d, seq), lambda b, g: (b, 0, 0))

    def q_spec(r):
        return pl.BlockSpec((tq, d), lambda b, g: (b * n_blocks + jnp.clip(n_blocks - 1 - g + r, 0, n_blocks - 1), 0))

    batch_rows = (TILES_IN_FLIGHT - 1) * tq + LATE_ROWS
    return pl.pallas_call(
        functools.partial(_attn_prompt_kernel, n_blocks=n_blocks),
        grid=(bsz, n_blocks + TILES_IN_FLIGHT),
        in_specs=[q_spec(0), q_spec(1), q_spec(2), q_spec(3), seq_spec, seq_spec,
                  pl.BlockSpec(tt.shape, lambda b, g: (0, 0))],
        out_specs=q_spec(TILES_IN_FLIGHT),
        out_shape=jax.ShapeDtypeStruct(q.shape, _BF16),
        scratch_shapes=(_state_scratch(TILES_IN_FLIGHT, tq, d) + _block_scratch(batch_rows, d)
                        + _block_scratch(batch_rows, d) + _block_scratch(tq, d)
                        + [pltpu.SMEM((2,), _F32)]),
        compiler_params=pltpu.CompilerParams(
            dimension_semantics=("arbitrary", "arbitrary"), vmem_limit_bytes=VMEM_LIMIT_BYTES),
        name="attn_prompt",
    )(q, q, q, q, kt, vt, tt)


def _attention_sample(q, kb_new, vb_new, cache_kt, cache_vt, tq):
    bsz, d, past = cache_kt.shape
    tt = _cumsum_matrix()
    new_spec = pl.BlockSpec((tq, d), lambda b: (b, 0))
    cache_spec = pl.BlockSpec((None, d, past), lambda b: (b, 0, 0))
    return pl.pallas_call(
        _attn_sample_kernel,
        grid=(bsz,),
        in_specs=[new_spec, new_spec, new_spec, cache_spec, cache_spec,
                  pl.BlockSpec(tt.shape, lambda b: (0, 0))],
        out_specs=new_spec,
        out_shape=jax.ShapeDtypeStruct(q.shape, _BF16),
        scratch_shapes=_state_scratch(1, tq, d) + _block_scratch(tq, d),
        compiler_params=pltpu.CompilerParams(
            dimension_semantics=("arbitrary",), vmem_limit_bytes=VMEM_LIMIT_BYTES),
        name="attn_sample",
    )(q, kb_new, vb_new, cache_kt, cache_vt, tt)


def _ffn_kernel(x_ref, mc_ref, ga_ref, at_ref, wo_ref, gf_ref, wu_ref, wd_ref, gl_ref, y_ref,
                h_ref, hn_ref, *, f_chunk, final_norm):
    mixed = mc_ref[...] + ga_ref[...] * at_ref[...]
    h = x_ref[...] + jnp.dot(mixed, wo_ref[...], preferred_element_type=_F32)
    h_ref[...] = h
    hn_ref[...] = (h * _rms_scale(h) * gf_ref[...]).astype(_BF16)
    d_ff = wu_ref.shape[1]
    acc = h_ref[...]
    for c in range(d_ff // f_chunk):
        cs = slice(c * f_chunk, (c + 1) * f_chunk)
        up = jnp.dot(hn_ref[...], wu_ref[:, cs], preferred_element_type=_F32)
        act = jnp.square(jnp.maximum(up, 0.0)).astype(_BF16)
        acc = acc + jnp.dot(act, wd_ref[cs, :], preferred_element_type=_F32)
    y_ref[...] = acc * _rms_scale(acc) * gl_ref[...] if final_norm else acc


def _merge_ffn(x2, mc, ga, attn, wo_bf, g_ffn, wu_bf, wd_bf, g_final, *, tm, final_norm):
    rows, d = x2.shape
    d_ff = wu_bf.shape[1]
    row_spec = pl.BlockSpec((tm, d), lambda i: (i, 0))
    const = lambda i: (0, 0)
    once = pl.Buffered(1)
    return pl.pallas_call(
        functools.partial(_ffn_kernel, f_chunk=d, final_norm=final_norm),
        grid=(rows // tm,),
        in_specs=[row_spec, row_spec, row_spec, row_spec,
                  pl.BlockSpec((d, d), const, pipeline_mode=once),
                  pl.BlockSpec((1, d), const),
                  pl.BlockSpec((d, d_ff), const, pipeline_mode=once),
                  pl.BlockSpec((d_ff, d), const, pipeline_mode=once),
                  pl.BlockSpec((1, d), const)],
        out_specs=row_spec,
        out_shape=jax.ShapeDtypeStruct((rows, d), _F32),
        scratch_shapes=[pltpu.VMEM((tm, d), _F32), pltpu.VMEM((tm, d), _BF16)],
        compiler_params=pltpu.CompilerParams(
            dimension_semantics=("arbitrary",), vmem_limit_bytes=VMEM_LIMIT_BYTES),
        name="merge_ffn",
    )(x2, mc, ga, attn, wo_bf, g_ffn.reshape(1, d), wu_bf, wd_bf, g_final.reshape(1, d))


def _row_tile(rows, limit=512):
    tm = min(rows, limit)
    assert rows % tm == 0 and tm % CONV_TAIL == 0
    return tm


def kernel(x_prompt, x_sample, cache_conv, cache_k, cache_v, g_mix, w_in, conv_w, w_out,
           g_ffn, w_up, w_down, g_final):
    depth = w_in.shape[0]
    bsz, seq, d = x_prompt.shape
    dbsz, dseq, _ = x_sample.shape
    past, n_heads, head_dim = cache_k.shape[2], cache_k.shape[3], cache_k.shape[4]
    assert n_heads * head_dim == d and head_dim * HEADS_PER_SLAB == LANES
    assert seq % KEY_BLOCK == 0 and seq // KEY_BLOCK >= TILES_IN_FLIGHT and past % KEY_BLOCK == 0
    assert CONV_WIDTH - 1 <= dseq <= KEY_BLOCK

    def feature_major(c):
        return jnp.transpose(c, (0, 2, 3, 1)).reshape(c.shape[0], d, c.shape[1])

    def position_major(ct):
        return jnp.transpose(ct, (0, 3, 1, 2))

    xp, xs = x_prompt, x_sample
    outs = {name: [] for name in ("conv_p", "k_p", "v_p", "conv_s", "k_s", "v_s")}
    for l in range(depth):
        w_bf = w_in[l].astype(_BF16)
        k0, k1 = 4 * d, 6 * d
        w_rest = jnp.concatenate([w_in[l][:, :k0], w_in[l][:, k1:]], axis=1).astype(_BF16)
        wkv_t = w_in[l][:, k0:k1].T.astype(_BF16)
        wo_bf = w_out[l].astype(_BF16)
        wu_bf = w_up[l].astype(_BF16)
        wd_bf = w_down[l].astype(_BF16)

        tm = _row_tile(seq)
        q, kt, vt, ktb, vtb, mc, ga, tail = _project(xp, g_mix[l], w_rest, conv_w[l], tm=tm, seg=tm,
                                                     n_heads=n_heads, wkv_t=wkv_t)
        attn = _attention_prompt(q, ktb, vtb)
        outs["conv_p"].append(tail[:, CONV_TAIL - (CONV_WIDTH - 1):, :])
        outs["k_p"].append(position_major(kt))
        outs["v_p"].append(position_major(vt))
        xp = _merge_ffn(xp.reshape(bsz * seq, d), mc, ga, attn, wo_bf, g_ffn[l], wu_bf, wd_bf,
                        g_final, tm=_row_tile(bsz * seq, FFN_ROWS), final_norm=l == depth - 1
                        ).reshape(bsz, seq, d)

        rows = dbsz * dseq
        hist = cache_conv[l].astype(_F32)
        prev1 = jnp.zeros((dbsz, dseq, d), _F32).at[:, 0].set(hist[:, 1]).reshape(rows, d)
        prev2 = (jnp.zeros((dbsz, dseq, d), _F32).at[:, 0].set(hist[:, 0]).at[:, 1].set(hist[:, 1])
                 .reshape(rows, d))
        q, k, v, kb, vb, mc, ga, u = _project(xs, g_mix[l], w_bf, conv_w[l], tm=rows, seg=dseq,
                                             n_heads=n_heads, prev=(prev1, prev2))
        attn = _attention_sample(q, kb, vb, feature_major(cache_k[l]), feature_major(cache_v[l]), dseq)
        outs["conv_s"].append(u.reshape(dbsz, dseq, d)[:, dseq - (CONV_WIDTH - 1):, :])
        outs["k_s"].append(k.reshape(dbsz, dseq, n_heads, head_dim))
        outs["v_s"].append(v.reshape(dbsz, dseq, n_heads, head_dim))
        xs = _merge_ffn(xs.reshape(rows, d), mc, ga, attn, wo_bf, g_ffn[l], wu_bf, wd_bf,
                        g_final, tm=_row_tile(rows), final_norm=l == depth - 1).reshape(dbsz, dseq, d)

    return (xp, xs, jnp.stack(outs["conv_p"]), jnp.stack(outs["k_p"]), jnp.stack(outs["v_p"]),
            jnp.stack(outs["conv_s"]), jnp.stack(outs["k_s"]), jnp.stack(outs["v_s"]))
```

```python
import functools

import jax
import jax.numpy as jnp
from jax import lax
from jax.experimental import pallas as pl
from jax.experimental.pallas import tpu as pltpu

RMS_EPS = 1e-6
CONV_WIDTH = 3
LANES = 128
KEY_BLOCK = 128
HEADS_PER_SLAB = 2
TILES_IN_FLIGHT = 3
FFN_ROWS = 1024
LATE_ROWS = 32
CONV_TAIL = 8
VMEM_LIMIT_BYTES = 60 * 1024 * 1024
CARRY_DONE = 104.0
LOG2_E = 1.4426950408889634
MASKED_SCORE = -1e30

_F32 = jnp.float32
_BF16 = jnp.bfloat16
_NT = (((1,), (1,)), ((), ()))


def _rms_scale(x32):
    return lax.rsqrt(jnp.mean(x32 * x32, axis=-1, keepdims=True) + RMS_EPS)


def _sigmoid(z):
    return 1.0 / (1.0 + jnp.exp(-z))


def _proj_kernel(*refs, d, tm, seg, scale):
    carried = seg == tm
    if carried:
        (x_ref, g_ref, w_ref, wkv_ref, cw_ref,
         q_ref, k_ref, v_ref, kb_ref, vb_ref, mc_ref, ga_ref, tail_ref,
         xn_ref, u_ref, conv_ref) = refs
    else:
        (x_ref, g_ref, w_ref, cw_ref, p1_ref, p2_ref,
         q_ref, k_ref, v_ref, kb_ref, vb_ref, mc_ref, ga_ref, tail_ref,
         xn_ref, u_ref, conv_ref) = refs
    b_g, c_g, h_g, q_g, k_g, v_g, gc_g, ga_g = range(8)

    if carried:
        @pl.when(pl.program_id(1) == 0)
        def _():
            u_ref[0:CONV_TAIL, :] = jnp.zeros((CONV_TAIL, d), _F32)

    x32 = x_ref[...]
    xn_ref[...] = (x32 * _rms_scale(x32) * g_ref[...]).astype(_BF16)

    def proj(group):
        return jnp.dot(xn_ref[...], w_ref[:, group * d:(group + 1) * d], preferred_element_type=_F32)

    u = proj(c_g) * proj(h_g)
    cw = cw_ref[...]
    if carried:
        u_ref[CONV_TAIL:CONV_TAIL + tm, :] = u
        with_history = u_ref[...]
        s1 = pltpu.roll(with_history, 1, 0)[CONV_TAIL:, :]
        s2 = pltpu.roll(with_history, 2, 0)[CONV_TAIL:, :]
        conv_ref[...] = cw[0:1, :] * s2 + cw[1:2, :] * s1 + cw[2:3, :] * u
        tail = u[tm - CONV_TAIL:tm, :]
        u_ref[0:CONV_TAIL, :] = tail
        tail_ref[...] = tail
    else:
        row = lax.broadcasted_iota(jnp.int32, (tm, d), 0) % seg
        s1 = jnp.where(row == 0, p1_ref[...], pltpu.roll(u, 1, 0))
        s2 = jnp.where(row < 2, p2_ref[...], pltpu.roll(u, 2, 0))
        conv_ref[...] = cw[0:1, :] * s2 + cw[1:2, :] * s1 + cw[2:3, :] * u
        tail_ref[...] = u

    conv_out = proj(b_g) * conv_ref[...]
    mc_ref[...] = (_sigmoid(proj(gc_g)) * conv_out).astype(_BF16)
    ga_ref[...] = _sigmoid(proj(ga_g)).astype(_BF16)
    q_ref[...] = (proj(q_g) * scale).astype(_BF16)
    if carried:
        for i, (o_ref, ob_ref) in enumerate(((k_ref, kb_ref), (v_ref, vb_ref))):
            t = lax.dot_general(wkv_ref[i * d:(i + 1) * d, :], xn_ref[...], _NT,
                                preferred_element_type=_F32)
            o_ref[...] = t.reshape(o_ref.shape)
            ob_ref[...] = t.astype(_BF16)
    else:
        for group, o_ref, ob_ref in ((k_g, k_ref, kb_ref), (v_g, v_ref, vb_ref)):
            t = proj(group)
            o_ref[...] = t.reshape(o_ref.shape)
            ob_ref[...] = t.astype(_BF16)


def _project(x, g, w_bf, conv_w, *, tm, seg, n_heads, wkv_t=None, prev=None):
    bsz, t, d = x.shape
    carried = prev is None
    n_t = t // tm if carried else 1
    rows = bsz * t
    x2 = x.reshape(rows, d)
    head_dim = d // n_heads
    scale = float(head_dim) ** -0.5

    if carried:
        grid = (bsz, n_t)
        row_map = lambda b, j: (b * n_t + j, 0)
    else:
        assert rows == tm
        grid = (1, 1)
        row_map = lambda b, j: (0, 0)
    const = lambda b, j: (0, 0)
    once = pl.Buffered(1)
    row_spec = pl.BlockSpec((tm, d), row_map)
    in_specs = [row_spec, pl.BlockSpec((1, d), const), pl.BlockSpec(w_bf.shape, const, pipeline_mode=once)]
    args = [x2, g.reshape(1, d), w_bf]
    if carried:
        in_specs.append(pl.BlockSpec(wkv_t.shape, const, pipeline_mode=once))
        args.append(wkv_t)
    in_specs.append(pl.BlockSpec((CONV_WIDTH, d), const))
    args.append(conv_w)
    bf = jax.ShapeDtypeStruct((rows, d), _BF16)
    if carried:
        kv = jax.ShapeDtypeStruct((bsz, n_heads, head_dim, t), _F32)
        kv_spec = pl.BlockSpec((None, n_heads, head_dim, tm), lambda b, j: (b, 0, 0, j))
        kvb = jax.ShapeDtypeStruct((bsz, d, t), _BF16)
        kvb_spec = pl.BlockSpec((None, d, tm), lambda b, j: (b, 0, j))
        tail_shape = jax.ShapeDtypeStruct((bsz, CONV_TAIL, d), _F32)
        tail_spec = pl.BlockSpec((None, CONV_TAIL, d), lambda b, j: (b, 0, 0))
    else:
        in_specs += [row_spec, row_spec]
        args += [prev[0], prev[1]]
        kv = jax.ShapeDtypeStruct((rows, n_heads, head_dim), _F32)
        kv_spec = pl.BlockSpec((tm, n_heads, head_dim), lambda b, j: (0, 0, 0))
        kvb, kvb_spec = bf, row_spec
        tail_shape = jax.ShapeDtypeStruct((rows, d), _F32)
        tail_spec = row_spec
    out_shape = [bf, kv, kv, kvb, kvb, bf, bf, tail_shape]
    out_specs = [row_spec, kv_spec, kv_spec, kvb_spec, kvb_spec, row_spec, row_spec, tail_spec]
    return pl.pallas_call(
        functools.partial(_proj_kernel, d=d, tm=tm, seg=seg, scale=scale),
        grid=grid, in_specs=in_specs, out_specs=out_specs, out_shape=out_shape,
        scratch_shapes=[pltpu.VMEM((tm, d), _BF16),
                        pltpu.VMEM((tm + CONV_TAIL, d), _F32),
                        pltpu.VMEM((tm, d), _F32)],
        compiler_params=pltpu.CompilerParams(
            dimension_semantics=("arbitrary", "arbitrary"), vmem_limit_bytes=VMEM_LIMIT_BYTES),
        name="proj_carried" if carried else "proj_segmented",
    )(*args)


def _stages(tiles, d, keys_on_lanes):
    n = len(tiles)
    starts = [sum(t[4] for t in tiles[:r]) for r in range(n + 1)]
    m = starts[n]
    n_slabs = d // LANES
    half = LANES // HEADS_PER_SLAB
    rows_per_slab = HEADS_PER_SLAB * m

    def _rows(x, r):
        return x[starts[r]:starts[r + 1]]

    def block_diagonal(blk, p):
        if keys_on_lanes:
            x = blk[p * LANES:(p + 1) * LANES, :]
            zero = jnp.zeros((half, KEY_BLOCK), x.dtype)
            return jnp.concatenate([jnp.concatenate([x[:half], zero], axis=1),
                                    jnp.concatenate([zero, x[half:]], axis=1)], axis=0)
        x = blk[:, p * LANES:(p + 1) * LANES]
        first_head = lax.broadcasted_iota(jnp.int32, x.shape, 1) < half
        return jnp.concatenate([jnp.where(first_head, x, jnp.zeros_like(x)),
                                jnp.where(first_head, jnp.zeros_like(x), x)], axis=0)

    def masked(s):
        pieces = [_rows(s, r) if t[2] is None else jnp.where(t[2], _rows(s, r), MASKED_SCORE)
                  for r, t in enumerate(tiles)]
        return jnp.concatenate(pieces, axis=0) if n > 1 else pieces[0]

    def scores_and_softplus(p, kblk, s_ref, hl_ref):
        sl = slice(p * LANES, (p + 1) * LANES)
        kbd = block_diagonal(kblk, p)
        q_rows = [t[0][t[3]:t[3] + t[4], sl] for t in tiles]
        q2 = jnp.concatenate(q_rows, axis=0) if n > 1 else q_rows[0]
        if keys_on_lanes:
            s = jnp.dot(q2, kbd, preferred_element_type=_F32)
        else:
            s = lax.dot_general(q2, kbd, _NT, preferred_element_type=_F32)
        s = masked(s)
        s_ref[p, 0:m, :] = s
        soft = jnp.maximum(s, 0.0) + jnp.log(1.0 + jnp.exp2(jnp.abs(s) * -LOG2_E))
        hi = soft.astype(_BF16)
        lo = (soft - hi.astype(_F32)).astype(_BF16)
        for h in range(HEADS_PER_SLAB):
            hs = slice(h * KEY_BLOCK, (h + 1) * KEY_BLOCK)
            r0 = (HEADS_PER_SLAB * p + h) * m
            hl_ref[r0:r0 + m, :] = jnp.concatenate([hi[:, hs], lo[:, hs]], axis=1)

    def weights_and_values(p, vblk, cr, carry_ref, oacc_ref, s_ref):
        sl = slice(p * LANES, (p + 1) * LANES)
        vbd = block_diagonal(vblk, p)
        per_head = [cr[h * m:(h + 1) * m] for h in range(HEADS_PER_SLAB)]
        csum = jnp.concatenate([c[:, :KEY_BLOCK] for c in per_head], axis=1)
        rsum = jnp.concatenate([c[:, KEY_BLOCK:] for c in per_head], axis=1)
        below = s_ref[p, 0:m, :] - csum
        args, carries = [], []
        for r, (_, slot, _, row0, rows, fresh) in enumerate(tiles):
            if fresh:
                args.append(_rows(below, r))
                carries.append(_rows(rsum, r))
            else:
                carry = carry_ref[slot, p, row0:row0 + rows]
                args.append(_rows(below, r) - carry)
                carries.append(carry + _rows(rsum, r))
        a = jnp.exp(jnp.concatenate(args, axis=0) if n > 1 else args[0]).astype(_BF16)
        if keys_on_lanes:
            o = lax.dot_general(a, vbd, _NT, preferred_element_type=_F32)
        else:
            o = jnp.dot(a, vbd, preferred_element_type=_F32)
        for r, (_, slot, _, row0, rows, fresh) in enumerate(tiles):
            if fresh:
                oacc_ref[slot, row0:row0 + rows, sl] = _rows(o, r)
            else:
                oacc_ref[slot, row0:row0 + rows, sl] += _rows(o, r)
            carry_ref[slot, p, row0:row0 + rows] = carries[r]
        late = carries[n - 2][LATE_ROWS:] if n > 1 else None
        return carries[n - 1], late

    def scores(kblk, s_ref, hl_ref):
        for p in range(n_slabs):
            scores_and_softplus(p, kblk, s_ref, hl_ref)

    def cumsum(hl_ref, tt_ref):
        return jnp.dot(hl_ref[0:n_slabs * rows_per_slab, :], tt_ref[...], preferred_element_type=_F32)

    def values(vblk, cr, carry_ref, oacc_ref, s_ref, before_slab=None):
        smallest = smallest_late = None
        for p in range(n_slabs):
            if before_slab is not None:
                before_slab(p)
            last, late = weights_and_values(p, vblk, cr[p * rows_per_slab:(p + 1) * rows_per_slab],
                                            carry_ref, oacc_ref, s_ref)
            smallest = last if smallest is None else jnp.minimum(smallest, last)
            if late is not None:
                smallest_late = late if smallest_late is None else jnp.minimum(smallest_late, late)
        return jnp.min(smallest), None if smallest_late is None else jnp.min(smallest_late)

    return scores, cumsum, values, scores_and_softplus


def _attend(tiles, kblk, vblk, keys_on_lanes, tt_ref, carry_ref, oacc_ref, s_ref, hl_ref):
    scores, cumsum, values, _ = _stages(tiles, kblk.shape[0 if keys_on_lanes else 1], keys_on_lanes)
    scores(kblk, s_ref, hl_ref)
    return values(vblk, cumsum(hl_ref, tt_ref), carry_ref, oacc_ref, s_ref)


def _causal_mask(tq):
    row = lax.broadcasted_iota(jnp.int32, (tq, KEY_BLOCK), 0)
    col = lax.broadcasted_iota(jnp.int32, (tq, KEY_BLOCK), 1)
    m = col < row
    return jnp.concatenate([m] * HEADS_PER_SLAB, axis=1)


def _walk_back(first_block, smallest, load_block, tile, tt_ref, carry_ref, oacc_ref, s_ref, hl_ref):
    def cond(state):
        kb, m = state
        return jnp.logical_and(kb >= 0, m <= CARRY_DONE)

    def body(state):
        kb, _ = state
        kblk, vblk = load_block(kb)
        m, _ = _attend([tile], kblk, vblk, True, tt_ref, carry_ref, oacc_ref, s_ref, hl_ref)
        return kb - 1, m

    lax.while_loop(cond, body, (jnp.asarray(first_block, jnp.int32), smallest))


def _attn_prompt_kernel(q0_ref, q1_ref, q2_ref, q3_ref, k_ref, v_ref, tt_ref, o_ref,
                        carry_ref, oacc_ref, sa_ref, hla_ref, sb_ref, hlb_ref, s1_ref, hl1_ref, m_ref,
                        *, n_blocks):
    g = pl.program_id(1)
    ja = n_blocks - 1 - g
    jb = ja + 1
    tq, d = q0_ref.shape
    slots = [lax.rem(jb + TILES_IN_FLIGHT + r, TILES_IN_FLIGHT) for r in range(TILES_IN_FLIGHT)]
    generic = (tt_ref, carry_ref, oacc_ref, s1_ref, hl1_ref)

    def load_block(kb):
        start = pl.multiple_of(jnp.clip(kb, 0, n_blocks - 1) * KEY_BLOCK, KEY_BLOCK)
        return k_ref[:, pl.ds(start, KEY_BLOCK)], v_ref[:, pl.ds(start, KEY_BLOCK)]

    @pl.when(g == 0)
    def _():
        carry_ref[...] = jnp.zeros(carry_ref.shape, _F32)
        oacc_ref[...] = jnp.zeros(oacc_ref.shape, _F32)
        sb_ref[...] = jnp.zeros(sb_ref.shape, _F32)
        hlb_ref[...] = jnp.zeros(hlb_ref.shape, _BF16)
        m_ref[0] = 0.0
        m_ref[1] = 0.0

    late_rows_done = m_ref[1] > CARRY_DONE
    last_tile_live = jb + TILES_IN_FLIGHT - 1 < n_blocks

    def step(s_new, hl_new, s_old, hl_old):
        mask = _causal_mask(tq)
        rows = ((0, tq, True), (0, tq, False), (0, LATE_ROWS, False))
        q_refs = (q0_ref, q1_ref, q2_ref)
        tiles_a = [(q_refs[r], None, mask if r == 0 else None) + rows[r] for r in range(TILES_IN_FLIGHT)]
        tiles_b = [(None, slots[r], mask if r == 0 else None) + rows[r] for r in range(TILES_IN_FLIGHT)]
        _, _, _, scores_slab_a = _stages(tiles_a, d, True)
        _, cumsum_b, values_b, _ = _stages(tiles_b, d, True)
        kblk = load_block(ja)[0]
        cr = cumsum_b(hl_old, tt_ref)
        m_ref[0], m_ref[1] = values_b(load_block(jb)[1], cr, carry_ref, oacc_ref, s_old,
                                      before_slab=lambda p: scores_slab_a(p, kblk, s_new, hl_new))

    @pl.when(g <= n_blocks)
    def _():
        even = lax.rem(g, 2) == 0
        pl.when(even)(lambda: step(sa_ref, hla_ref, sb_ref, hlb_ref))
        pl.when(jnp.logical_not(even))(lambda: step(sb_ref, hlb_ref, sa_ref, hla_ref))

    @pl.when(jnp.logical_and(jnp.logical_and(last_tile_live, jb >= 0), jnp.logical_not(late_rows_done)))
    def _():
        late_tile = (q3_ref, slots[2], None, LATE_ROWS, tq - LATE_ROWS, False)
        m_late, _ = _attend([late_tile], *load_block(jb), True, *generic)
        m_ref[0] = jnp.minimum(m_ref[0], m_late)

    first_block = jnp.where(last_tile_live, jb - 1, -1)
    _walk_back(first_block, m_ref[0], load_block, (q3_ref, slots[2], None, 0, tq, False), *generic)
    o_ref[...] = oacc_ref[slots[2]].astype(_BF16)


def _attn_sample_kernel(q_ref, kn_ref, vn_ref, kc_ref, vc_ref, tt_ref, o_ref,
                        carry_ref, oacc_ref, s_ref, hl_ref):
    tq, d = q_ref.shape
    past = kc_ref.shape[1]
    pad = jnp.zeros((KEY_BLOCK - tq, d), _BF16)
    kblk = jnp.concatenate([kn_ref[...], pad], axis=0)
    vblk = jnp.concatenate([vn_ref[...], pad], axis=0)
    m, _ = _attend([(q_ref, 0, _causal_mask(tq), 0, tq, True)], kblk, vblk, False,
                   tt_ref, carry_ref, oacc_ref, s_ref, hl_ref)

    def load_block(kb):
        start = pl.multiple_of(kb * KEY_BLOCK, KEY_BLOCK)
        return (kc_ref[:, pl.ds(start, KEY_BLOCK)].astype(_BF16),
                vc_ref[:, pl.ds(start, KEY_BLOCK)].astype(_BF16))

    _walk_back(past // KEY_BLOCK - 1, m, load_block, (q_ref, 0, None, 0, tq, False),
               tt_ref, carry_ref, oacc_ref, s_ref, hl_ref)
    o_ref[...] = oacc_ref[0].astype(_BF16)


def _cumsum_matrix():
    j = lax.broadcasted_iota(jnp.int32, (KEY_BLOCK, KEY_BLOCK), 0)
    s = lax.broadcasted_iota(jnp.int32, (KEY_BLOCK, KEY_BLOCK), 1)
    half = jnp.concatenate([(j >= s).astype(_BF16), jnp.ones((KEY_BLOCK, KEY_BLOCK), _BF16)], axis=1)
    return jnp.concatenate([half, half], axis=0)


def _state_scratch(n_tiles, tq, d):
    width = HEADS_PER_SLAB * KEY_BLOCK
    return [pltpu.VMEM((n_tiles, d // LANES, tq, width), _F32),
            pltpu.VMEM((n_tiles, tq, d), _F32)]


def _block_scratch(rows, d):
    n_slabs = d // LANES
    width = HEADS_PER_SLAB * KEY_BLOCK
    return [pltpu.VMEM((n_slabs, rows, width), _F32),
            pltpu.VMEM((HEADS_PER_SLAB * n_slabs * rows, width), _BF16)]


def _attention_prompt(q, kt, vt):
    bsz, d, seq = kt.shape
    tq = KEY_BLOCK
    n_blocks = seq // tq
    tt = _cumsum_matrix()
    seq_spec = pl.BlockSpec((None, d, seq), lambda b, g: (b, 0, 0))

    def q_spec(r):
        return pl.BlockSpec((tq, d), lambda b, g: (b * n_blocks + jnp.clip(n_blocks - 1 - g + r, 0, n_blocks - 1), 0))

    batch_rows = (TILES_IN_FLIGHT - 1) * tq + LATE_ROWS
    return pl.pallas_call(
        functools.partial(_attn_prompt_kernel, n_blocks=n_blocks),
        grid=(bsz, n_blocks + TILES_IN_FLIGHT),
        in_specs=[q_spec(0), q_spec(1), q_spec(2), q_spec(3), seq_spec, seq_spec,
                  pl.BlockSpec(tt.shape, lambda b, g: (0, 0))],
        out_specs=q_spec(TILES_IN_FLIGHT),
        out_shape=jax.ShapeDtypeStruct(q.shape, _BF16),
        scratch_shapes=(_state_scratch(TILES_IN_FLIGHT, tq, d) + _block_scratch(batch_rows, d)
                        + _block_scratch(batch_rows, d) + _block_scratch(tq, d)
                        + [pltpu.SMEM((2,), _F32)]),
        compiler_params=pltpu.CompilerParams(
            dimension_semantics=("arbitrary", "arbitrary"), vmem_limit_bytes=VMEM_LIMIT_BYTES),
        name="attn_prompt",
    )(q, q, q, q, kt, vt, tt)


def _attention_sample(q, kb_new, vb_new, cache_kt, cache_vt, tq):
    bsz, d, past = cache_kt.shape
    tt = _cumsum_matrix()
    new_spec = pl.BlockSpec((tq, d), lambda b: (b, 0))
    cache_spec = pl.BlockSpec((None, d, past), lambda b: (b, 0, 0))
    return pl.pallas_call(
        _attn_sample_kernel,
        grid=(bsz,),
        in_specs=[new_spec, new_spec, new_spec, cache_spec, cache_spec,
                  pl.BlockSpec(tt.shape, lambda b: (0, 0))],
        out_specs=new_spec,
        out_shape=jax.ShapeDtypeStruct(q.shape, _BF16),
        scratch_shapes=_state_scratch(1, tq, d) + _block_scratch(tq, d),
        compiler_params=pltpu.CompilerParams(
            dimension_semantics=("arbitrary",), vmem_limit_bytes=VMEM_LIMIT_BYTES),
        name="attn_sample",
    )(q, kb_new, vb_new, cache_kt, cache_vt, tt)


def _ffn_kernel(x_ref, mc_ref, ga_ref, at_ref, wo_ref, gf_ref, wu_ref, wd_ref, gl_ref, y_ref,
                h_ref, hn_ref, *, f_chunk, final_norm):
    mixed = mc_ref[...] + ga_ref[...] * at_ref[...]
    h = x_ref[...] + jnp.dot(mixed, wo_ref[...], preferred_element_type=_F32)
    h_ref[...] = h
    hn_ref[...] = (h * _rms_scale(h) * gf_ref[...]).astype(_BF16)
    d_ff = wu_ref.shape[1]
    acc = h_ref[...]
    for c in range(d_ff // f_chunk):
        cs = slice(c * f_chunk, (c + 1) * f_chunk)
        up = jnp.dot(hn_ref[...], wu_ref[:, cs], preferred_element_type=_F32)
        act = jnp.square(jnp.maximum(up, 0.0)).astype(_BF16)
        acc = acc + jnp.dot(act, wd_ref[cs, :], preferred_element_type=_F32)
    y_ref[...] = acc * _rms_scale(acc) * gl_ref[...] if final_norm else acc


def _merge_ffn(x2, mc, ga, attn, wo_bf, g_ffn, wu_bf, wd_bf, g_final, *, tm, final_norm):
    rows, d = x2.shape
    d_ff = wu_bf.shape[1]
    row_spec = pl.BlockSpec((tm, d), lambda i: (i, 0))
    const = lambda i: (0, 0)
    once = pl.Buffered(1)
    return pl.pallas_call(
        functools.partial(_ffn_kernel, f_chunk=d, final_norm=final_norm),
        grid=(rows // tm,),
        in_specs=[row_spec, row_spec, row_spec, row_spec,
                  pl.BlockSpec((d, d), const, pipeline_mode=once),
                  pl.BlockSpec((1, d), const),
                  pl.BlockSpec((d, d_ff), const, pipeline_mode=once),
                  pl.BlockSpec((d_ff, d), const, pipeline_mode=once),
                  pl.BlockSpec((1, d), const)],
        out_specs=row_spec,
        out_shape=jax.ShapeDtypeStruct((rows, d), _F32),
        scratch_shapes=[pltpu.VMEM((tm, d), _F32), pltpu.VMEM((tm, d), _BF16)],
        compiler_params=pltpu.CompilerParams(
            dimension_semantics=("arbitrary",), vmem_limit_bytes=VMEM_LIMIT_BYTES),
        name="merge_ffn",
    )(x2, mc, ga, attn, wo_bf, g_ffn.reshape(1, d), wu_bf, wd_bf, g_final.reshape(1, d))


def _row_tile(rows, limit=512):
    tm = min(rows, limit)
    assert rows % tm == 0 and tm % CONV_TAIL == 0
    return tm


def kernel(x_prompt, x_sample, cache_conv, cache_k, cache_v, g_mix, w_in, conv_w, w_out,
           g_ffn, w_up, w_down, g_final):
    depth = w_in.shape[0]
    bsz, seq, d = x_prompt.shape
    dbsz, dseq, _ = x_sample.shape
    past, n_heads, head_dim = cache_k.shape[2], cache_k.shape[3], cache_k.shape[4]
    assert n_heads * head_dim == d and head_dim * HEADS_PER_SLAB == LANES
    assert seq % KEY_BLOCK == 0 and seq // KEY_BLOCK >= TILES_IN_FLIGHT and past % KEY_BLOCK == 0
    assert CONV_WIDTH - 1 <= dseq <= KEY_BLOCK

    def feature_major(c):
        return jnp.transpose(c, (0, 2, 3, 1)).reshape(c.shape[0], d, c.shape[1])

    def position_major(ct):
        return jnp.transpose(ct, (0, 3, 1, 2))

    xp, xs = x_prompt, x_sample
    outs = {name: [] for name in ("conv_p", "k_p", "v_p", "conv_s", "k_s", "v_s")}
    for l in range(depth):
        w_bf = w_in[l].astype(_BF16)
        wkv_t = w_bf[:, 4 * d:6 * d].T
        wo_bf = w_out[l].astype(_BF16)
        wu_bf = w_up[l].astype(_BF16)
        wd_bf = w_down[l].astype(_BF16)

        tm = _row_tile(seq)
        q, kt, vt, ktb, vtb, mc, ga, tail = _project(xp, g_mix[l], w_bf, conv_w[l], tm=tm, seg=tm,
                                                     n_heads=n_heads, wkv_t=wkv_t)
        attn = _attention_prompt(q, ktb, vtb)
        outs["conv_p"].append(tail[:, CONV_TAIL - (CONV_WIDTH - 1):, :])
        outs["k_p"].append(position_major(kt))
        outs["v_p"].append(position_major(vt))
        xp = _merge_ffn(xp.reshape(bsz * seq, d), mc, ga, attn, wo_bf, g_ffn[l], wu_bf, wd_bf,
                        g_final, tm=_row_tile(bsz * seq, FFN_ROWS), final_norm=l == depth - 1
                        ).reshape(bsz, seq, d)

        rows = dbsz * dseq
        hist = cache_conv[l].astype(_F32)
        prev1 = jnp.zeros((dbsz, dseq, d), _F32).at[:, 0].set(hist[:, 1]).reshape(rows, d)
        prev2 = (jnp.zeros((dbsz, dseq, d), _F32).at[:, 0].set(hist[:, 0]).at[:, 1].set(hist[:, 1])
                 .reshape(rows, d))
        q, k, v, kb, vb, mc, ga, u = _project(xs, g_mix[l], w_bf, conv_w[l], tm=rows, seg=dseq,
                                             n_heads=n_heads, prev=(prev1, prev2))
        attn = _attention_sample(q, kb, vb, feature_major(cache_k[l]), feature_major(cache_v[l]), dseq)
        outs["conv_s"].append(u.reshape(dbsz, dseq, d)[:, dseq - (CONV_WIDTH - 1):, :])
        outs["k_s"].append(k.reshape(dbsz, dseq, n_heads, head_dim))
        outs["v_s"].append(v.reshape(dbsz, dseq, n_heads, head_dim))
        xs = _merge_ffn(xs.reshape(rows, d), mc, ga, attn, wo_bf, g_ffn[l], wu_bf, wd_bf,
                        g_final, tm=_row_tile(rows), final_norm=l == depth - 1).reshape(dbsz, dseq, d)

    return (xp, xs, jnp.stack(outs["conv_p"]), jnp.stack(outs["k_p"]), jnp.stack(outs["v_p"]),
            jnp.stack(outs["conv_s"]), jnp.stack(outs["k_s"]), jnp.stack(outs["v_s"]))
```

```python
import functools

import jax
import jax.numpy as jnp
from jax import lax
from jax.experimental import pallas as pl
from jax.experimental.pallas import tpu as pltpu

RMS_EPS = 1e-6
CONV_WIDTH = 3
LANES = 128
KEY_BLOCK = 128
HEADS_PER_SLAB = 2
TILES_IN_FLIGHT = 3
FFN_ROWS = 1024
LATE_ROWS = 32
CONV_TAIL = 8
VMEM_LIMIT_BYTES = 60 * 1024 * 1024
CARRY_DONE = 104.0
LOG2_E = 1.4426950408889634
MASKED_SCORE = -1e30

_F32 = jnp.float32
_BF16 = jnp.bfloat16
_NT = (((1,), (1,)), ((), ()))


def _rms_scale(x32):
    return lax.rsqrt(jnp.mean(x32 * x32, axis=-1, keepdims=True) + RMS_EPS)


def _sigmoid(z):
    return 1.0 / (1.0 + jnp.exp(-z))


def _proj_kernel(*refs, d, tm, seg, scale):
    carried = seg == tm
    if carried:
        (x_ref, g_ref, w_ref, wkv_ref, cw_ref,
         q_ref, k_ref, v_ref, kb_ref, vb_ref, mc_ref, ga_ref, tail_ref,
         xn_ref, u_ref, conv_ref) = refs
    else:
        (x_ref, g_ref, w_ref, cw_ref, p1_ref, p2_ref,
         q_ref, k_ref, v_ref, kb_ref, vb_ref, mc_ref, ga_ref, tail_ref,
         xn_ref, u_ref, conv_ref) = refs
    b_g, c_g, h_g, q_g, k_g, v_g, gc_g, ga_g = range(8)

    if carried:
        @pl.when(pl.program_id(1) == 0)
        def _():
            u_ref[0:CONV_TAIL, :] = jnp.zeros((CONV_TAIL, d), _F32)

    x32 = x_ref[...]
    xn_ref[...] = (x32 * _rms_scale(x32) * g_ref[...]).astype(_BF16)

    def proj(group):
        return jnp.dot(xn_ref[...], w_ref[:, group * d:(group + 1) * d], preferred_element_type=_F32)

    u = proj(c_g) * proj(h_g)
    cw = cw_ref[...]
    if carried:
        u_ref[CONV_TAIL:CONV_TAIL + tm, :] = u
        with_history = u_ref[...]
        s1 = pltpu.roll(with_history, 1, 0)[CONV_TAIL:, :]
        s2 = pltpu.roll(with_history, 2, 0)[CONV_TAIL:, :]
        conv_ref[...] = cw[0:1, :] * s2 + cw[1:2, :] * s1 + cw[2:3, :] * u
        tail = u[tm - CONV_TAIL:tm, :]
        u_ref[0:CONV_TAIL, :] = tail
        tail_ref[...] = tail
    else:
        row = lax.broadcasted_iota(jnp.int32, (tm, d), 0) % seg
        s1 = jnp.where(row == 0, p1_ref[...], pltpu.roll(u, 1, 0))
        s2 = jnp.where(row < 2, p2_ref[...], pltpu.roll(u, 2, 0))
        conv_ref[...] = cw[0:1, :] * s2 + cw[1:2, :] * s1 + cw[2:3, :] * u
        tail_ref[...] = u

    conv_out = proj(b_g) * conv_ref[...]
    mc_ref[...] = (_sigmoid(proj(gc_g)) * conv_out).astype(_BF16)
    ga_ref[...] = _sigmoid(proj(ga_g)).astype(_BF16)
    q_ref[...] = (proj(q_g) * scale).astype(_BF16)
    if carried:
        for i, (o_ref, ob_ref) in enumerate(((k_ref, kb_ref), (v_ref, vb_ref))):
            t = lax.dot_general(wkv_ref[i * d:(i + 1) * d, :], xn_ref[...], _NT,
                                preferred_element_type=_F32)
            o_ref[...] = t.reshape(o_ref.shape)
            ob_ref[...] = t.astype(_BF16)
    else:
        for group, o_ref, ob_ref in ((k_g, k_ref, kb_ref), (v_g, v_ref, vb_ref)):
            t = proj(group)
            o_ref[...] = t.reshape(o_ref.shape)
            ob_ref[...] = t.astype(_BF16)


def _project(x, g, w_bf, conv_w, *, tm, seg, n_heads, wkv_t=None, prev=None):
    bsz, t, d = x.shape
    carried = prev is None
    n_t = t // tm if carried else 1
    rows = bsz * t
    x2 = x.reshape(rows, d)
    head_dim = d // n_heads
    scale = float(head_dim) ** -0.5

    if carried:
        grid = (bsz, n_t)
        row_map = lambda b, j: (b * n_t + j, 0)
    else:
        assert rows == tm
        grid = (1, 1)
        row_map = lambda b, j: (0, 0)
    const = lambda b, j: (0, 0)
    once = pl.Buffered(1)
    row_spec = pl.BlockSpec((tm, d), row_map)
    in_specs = [row_spec, pl.BlockSpec((1, d), const), pl.BlockSpec(w_bf.shape, const, pipeline_mode=once)]
    args = [x2, g.reshape(1, d), w_bf]
    if carried:
        in_specs.append(pl.BlockSpec(wkv_t.shape, const, pipeline_mode=once))
        args.append(wkv_t)
    in_specs.append(pl.BlockSpec((CONV_WIDTH, d), const))
    args.append(conv_w)
    bf = jax.ShapeDtypeStruct((rows, d), _BF16)
    if carried:
        kv = jax.ShapeDtypeStruct((bsz, n_heads, head_dim, t), _F32)
        kv_spec = pl.BlockSpec((None, n_heads, head_dim, tm), lambda b, j: (b, 0, 0, j))
        kvb = jax.ShapeDtypeStruct((bsz, d, t), _BF16)
        kvb_spec = pl.BlockSpec((None, d, tm), lambda b, j: (b, 0, j))
        tail_shape = jax.ShapeDtypeStruct((bsz, CONV_TAIL, d), _F32)
        tail_spec = pl.BlockSpec((None, CONV_TAIL, d), lambda b, j: (b, 0, 0))
    else:
        in_specs += [row_spec, row_spec]
        args += [prev[0], prev[1]]
        kv = jax.ShapeDtypeStruct((rows, n_heads, head_dim), _F32)
        kv_spec = pl.BlockSpec((tm, n_heads, head_dim), lambda b, j: (0, 0, 0))
        kvb, kvb_spec = bf, row_spec
        tail_shape = jax.ShapeDtypeStruct((rows, d), _F32)
        tail_spec = row_spec
    out_shape = [bf, kv, kv, kvb, kvb, bf, bf, tail_shape]
    out_specs = [row_spec, kv_spec, kv_spec, kvb_spec, kvb_spec, row_spec, row_spec, tail_spec]
    return pl.pallas_call(
        functools.partial(_proj_kernel, d=d, tm=tm, seg=seg, scale=scale),
        grid=grid, in_specs=in_specs, out_specs=out_specs, out_shape=out_shape,
        scratch_shapes=[pltpu.VMEM((tm, d), _BF16),
                        pltpu.VMEM((tm + CONV_TAIL, d), _F32),
                        pltpu.VMEM((tm, d), _F32)],
        compiler_params=pltpu.CompilerParams(
            dimension_semantics=("arbitrary", "arbitrary"), vmem_limit_bytes=VMEM_LIMIT_BYTES),
        name="proj_carried" if carried else "proj_segmented",
    )(*args)


def _stages(tiles, d, keys_on_lanes):
    n = len(tiles)
    starts = [sum(t[4] for t in tiles[:r]) for r in range(n + 1)]
    m = starts[n]
    n_slabs = d // LANES
    half = LANES // HEADS_PER_SLAB
    rows_per_slab = HEADS_PER_SLAB * m

    def _rows(x, r):
        return x[starts[r]:starts[r + 1]]

    def block_diagonal(blk, p):
        if keys_on_lanes:
            x = blk[p * LANES:(p + 1) * LANES, :]
            zero = jnp.zeros((half, KEY_BLOCK), x.dtype)
            return jnp.concatenate([jnp.concatenate([x[:half], zero], axis=1),
                                    jnp.concatenate([zero, x[half:]], axis=1)], axis=0)
        x = blk[:, p * LANES:(p + 1) * LANES]
        first_head = lax.broadcasted_iota(jnp.int32, x.shape, 1) < half
        return jnp.concatenate([jnp.where(first_head, x, jnp.zeros_like(x)),
                                jnp.where(first_head, jnp.zeros_like(x), x)], axis=0)

    def masked(s):
        pieces = [_rows(s, r) if t[2] is None else jnp.where(t[2], _rows(s, r), MASKED_SCORE)
                  for r, t in enumerate(tiles)]
        return jnp.concatenate(pieces, axis=0) if n > 1 else pieces[0]

    def scores_and_softplus(p, kblk, s_ref, hl_ref):
        sl = slice(p * LANES, (p + 1) * LANES)
        kbd = block_diagonal(kblk, p)
        q_rows = [t[0][t[3]:t[3] + t[4], sl] for t in tiles]
        q2 = jnp.concatenate(q_rows, axis=0) if n > 1 else q_rows[0]
        if keys_on_lanes:
            s = jnp.dot(q2, kbd, preferred_element_type=_F32)
        else:
            s = lax.dot_general(q2, kbd, _NT, preferred_element_type=_F32)
        s = masked(s)
        s_ref[p, 0:m, :] = s
        soft = jnp.maximum(s, 0.0) + jnp.log(1.0 + jnp.exp2(jnp.abs(s) * -LOG2_E))
        hi = soft.astype(_BF16)
        lo = (soft - hi.astype(_F32)).astype(_BF16)
        for h in range(HEADS_PER_SLAB):
            hs = slice(h * KEY_BLOCK, (h + 1) * KEY_BLOCK)
            r0 = (HEADS_PER_SLAB * p + h) * m
            hl_ref[r0:r0 + m, :] = jnp.concatenate([hi[:, hs], lo[:, hs]], axis=1)

    def weights_and_values(p, vblk, cr, carry_ref, oacc_ref, s_ref):
        sl = slice(p * LANES, (p + 1) * LANES)
        vbd = block_diagonal(vblk, p)
        per_head = [cr[h * m:(h + 1) * m] for h in range(HEADS_PER_SLAB)]
        csum = jnp.concatenate([c[:, :KEY_BLOCK] for c in per_head], axis=1)
        rsum = jnp.concatenate([c[:, KEY_BLOCK:] for c in per_head], axis=1)
        below = s_ref[p, 0:m, :] - csum
        args, carries = [], []
        for r, (_, slot, _, row0, rows, fresh) in enumerate(tiles):
            if fresh:
                args.append(_rows(below, r))
                carries.append(_rows(rsum, r))
            else:
                carry = carry_ref[slot, p, row0:row0 + rows]
                args.append(_rows(below, r) - carry)
                carries.append(carry + _rows(rsum, r))
        a = jnp.exp(jnp.concatenate(args, axis=0) if n > 1 else args[0]).astype(_BF16)
        if keys_on_lanes:
            o = lax.dot_general(a, vbd, _NT, preferred_element_type=_F32)
        else:
            o = jnp.dot(a, vbd, preferred_element_type=_F32)
        for r, (_, slot, _, row0, rows, fresh) in enumerate(tiles):
            if fresh:
                oacc_ref[slot, row0:row0 + rows, sl] = _rows(o, r)
            else:
                oacc_ref[slot, row0:row0 + rows, sl] += _rows(o, r)
            carry_ref[slot, p, row0:row0 + rows] = carries[r]
        late = carries[n - 2][LATE_ROWS:] if n > 1 else None
        return carries[n - 1], late

    def scores(kblk, s_ref, hl_ref):
        for p in range(n_slabs):
            scores_and_softplus(p, kblk, s_ref, hl_ref)

    def cumsum(hl_ref, tt_ref):
        cr = jnp.dot(hl_ref[0:n_slabs * rows_per_slab, :], tt_ref[...], preferred_element_type=_F32)
        return lambda p: cr[p * rows_per_slab:(p + 1) * rows_per_slab]

    def values(vblk, cr_of_slab, carry_ref, oacc_ref, s_ref, before_slab=None):
        smallest = smallest_late = None
        for p in range(n_slabs):
            if before_slab is not None:
                before_slab(p)
            last, late = weights_and_values(p, vblk, cr_of_slab(p), carry_ref, oacc_ref, s_ref)
            smallest = last if smallest is None else jnp.minimum(smallest, last)
            if late is not None:
                smallest_late = late if smallest_late is None else jnp.minimum(smallest_late, late)
        return jnp.min(smallest), None if smallest_late is None else jnp.min(smallest_late)

    return scores, cumsum, values, scores_and_softplus


def _attend(tiles, kblk, vblk, keys_on_lanes, tt_ref, carry_ref, oacc_ref, s_ref, hl_ref):
    scores, cumsum, values, _ = _stages(tiles, kblk.shape[0 if keys_on_lanes else 1], keys_on_lanes)
    scores(kblk, s_ref, hl_ref)
    return values(vblk, cumsum(hl_ref, tt_ref), carry_ref, oacc_ref, s_ref)


def _causal_mask(tq):
    row = lax.broadcasted_iota(jnp.int32, (tq, KEY_BLOCK), 0)
    col = lax.broadcasted_iota(jnp.int32, (tq, KEY_BLOCK), 1)
    m = col < row
    return jnp.concatenate([m] * HEADS_PER_SLAB, axis=1)


def _walk_back(first_block, smallest, load_block, tile, tt_ref, carry_ref, oacc_ref, s_ref, hl_ref):
    def cond(state):
        kb, m = state
        return jnp.logical_and(kb >= 0, m <= CARRY_DONE)

    def body(state):
        kb, _ = state
        kblk, vblk = load_block(kb)
        m, _ = _attend([tile], kblk, vblk, True, tt_ref, carry_ref, oacc_ref, s_ref, hl_ref)
        return kb - 1, m

    lax.while_loop(cond, body, (jnp.asarray(first_block, jnp.int32), smallest))


def _attn_prompt_kernel(q0_ref, q1_ref, q2_ref, q3_ref, k_ref, v_ref, tt_ref, o_ref,
                        carry_ref, oacc_ref, sa_ref, hla_ref, sb_ref, hlb_ref, s1_ref, hl1_ref, m_ref,
                        *, n_blocks):
    g = pl.program_id(1)
    ja = n_blocks - 1 - g
    jb = ja + 1
    tq, d = q0_ref.shape
    slots = [lax.rem(jb + TILES_IN_FLIGHT + r, TILES_IN_FLIGHT) for r in range(TILES_IN_FLIGHT)]
    generic = (tt_ref, carry_ref, oacc_ref, s1_ref, hl1_ref)

    def load_block(kb):
        start = pl.multiple_of(jnp.clip(kb, 0, n_blocks - 1) * KEY_BLOCK, KEY_BLOCK)
        return k_ref[:, pl.ds(start, KEY_BLOCK)], v_ref[:, pl.ds(start, KEY_BLOCK)]

    @pl.when(g == 0)
    def _():
        carry_ref[...] = jnp.zeros(carry_ref.shape, _F32)
        oacc_ref[...] = jnp.zeros(oacc_ref.shape, _F32)
        m_ref[0] = 0.0
        m_ref[1] = 0.0

    late_rows_done = m_ref[1] > CARRY_DONE
    last_tile_live = jb + TILES_IN_FLIGHT - 1 < n_blocks
    buffers = ((sa_ref, hla_ref), (sb_ref, hlb_ref))

    def step(new, old):
        mask = _causal_mask(tq)
        rows = ((0, tq, True), (0, tq, False), (0, LATE_ROWS, False))
        q_refs = (q0_ref, q1_ref, q2_ref)
        tiles_a = [(q_refs[r], None, mask if r == 0 else None) + rows[r] for r in range(TILES_IN_FLIGHT)]
        tiles_b = [(None, slots[r], mask if r == 0 else None) + rows[r] for r in range(TILES_IN_FLIGHT)]
        scores_a, _, _, scores_slab_a = _stages(tiles_a, d, True)
        _, cumsum_b, values_b, _ = _stages(tiles_b, d, True)
        kblk = load_block(ja)[0]
        if old is None:
            scores_a(kblk, *new)
            return
        between = None if new is None else (lambda p: scores_slab_a(p, kblk, *new))
        m_ref[0], m_ref[1] = values_b(load_block(jb)[1], cumsum_b(old[1], tt_ref), carry_ref, oacc_ref,
                                      old[0], before_slab=between)

    pl.when(g == 0)(lambda: step(buffers[0], None))
    pl.when(g == n_blocks)(lambda: step(None, buffers[(n_blocks - 1) % 2]))

    @pl.when(jnp.logical_and(g > 0, g < n_blocks))
    def _():
        even = lax.rem(g, 2) == 0
        pl.when(even)(lambda: step(buffers[0], buffers[1]))
        pl.when(jnp.logical_not(even))(lambda: step(buffers[1], buffers[0]))

    @pl.when(jnp.logical_and(jnp.logical_and(last_tile_live, jb >= 0), jnp.logical_not(late_rows_done)))
    def _():
        late_tile = (q3_ref, slots[2], None, LATE_ROWS, tq - LATE_ROWS, False)
        m_late, _ = _attend([late_tile], *load_block(jb), True, *generic)
        m_ref[0] = jnp.minimum(m_ref[0], m_late)

    first_block = jnp.where(last_tile_live, jb - 1, -1)
    _walk_back(first_block, m_ref[0], load_block, (q3_ref, slots[2], None, 0, tq, False), *generic)
    o_ref[...] = oacc_ref[slots[2]].astype(_BF16)


def _attn_sample_kernel(q_ref, kn_ref, vn_ref, kc_ref, vc_ref, tt_ref, o_ref,
                        carry_ref, oacc_ref, s_ref, hl_ref):
    tq, d = q_ref.shape
    past = kc_ref.shape[1]
    pad = jnp.zeros((KEY_BLOCK - tq, d), _BF16)
    kblk = jnp.concatenate([kn_ref[...], pad], axis=0)
    vblk = jnp.concatenate([vn_ref[...], pad], axis=0)
    m, _ = _attend([(q_ref, 0, _causal_mask(tq), 0, tq, True)], kblk, vblk, False,
                   tt_ref, carry_ref, oacc_ref, s_ref, hl_ref)

    def load_block(kb):
        start = pl.multiple_of(kb * KEY_BLOCK, KEY_BLOCK)
        return (kc_ref[:, pl.ds(start, KEY_BLOCK)].astype(_BF16),
                vc_ref[:, pl.ds(start, KEY_BLOCK)].astype(_BF16))

    _walk_back(past // KEY_BLOCK - 1, m, load_block, (q_ref, 0, None, 0, tq, False),
               tt_ref, carry_ref, oacc_ref, s_ref, hl_ref)
    o_ref[...] = oacc_ref[0].astype(_BF16)


def _cumsum_matrix():
    j = lax.broadcasted_iota(jnp.int32, (KEY_BLOCK, KEY_BLOCK), 0)
    s = lax.broadcasted_iota(jnp.int32, (KEY_BLOCK, KEY_BLOCK), 1)
    half = jnp.concatenate([(j >= s).astype(_BF16), jnp.ones((KEY_BLOCK, KEY_BLOCK), _BF16)], axis=1)
    return jnp.concatenate([half, half], axis=0)


def _state_scratch(n_tiles, tq, d):
    width = HEADS_PER_SLAB * KEY_BLOCK
    return [pltpu.VMEM((n_tiles, d // LANES, tq, width), _F32),
            pltpu.VMEM((n_tiles, tq, d), _F32)]


def _block_scratch(rows, d):
    n_slabs = d // LANES
    width = HEADS_PER_SLAB * KEY_BLOCK
    return [pltpu.VMEM((n_slabs, rows, width), _F32),
            pltpu.VMEM((HEADS_PER_SLAB * n_slabs * rows, width), _BF16)]


def _attention_prompt(q, kt, vt):
    bsz, d, seq = kt.shape
    tq = KEY_BLOCK
    n_blocks = seq // tq
    tt = _cumsum_matrix()
    seq_spec = pl.BlockSpec((None, d, seq), lambda b, g: (b, 0, 0))

    def q_spec(r):
        return pl.BlockSpec((tq, d), lambda b, g: (b * n_blocks + jnp.clip(n_blocks - 1 - g + r, 0, n_blocks - 1), 0))

    batch_rows = (TILES_IN_FLIGHT - 1) * tq + LATE_ROWS
    return pl.pallas_call(
        functools.partial(_attn_prompt_kernel, n_blocks=n_blocks),
        grid=(bsz, n_blocks + TILES_IN_FLIGHT),
        in_specs=[q_spec(0), q_spec(1), q_spec(2), q_spec(3), seq_spec, seq_spec,
                  pl.BlockSpec(tt.shape, lambda b, g: (0, 0))],
        out_specs=q_spec(TILES_IN_FLIGHT),
        out_shape=jax.ShapeDtypeStruct(q.shape, _BF16),
        scratch_shapes=(_state_scratch(TILES_IN_FLIGHT, tq, d) + _block_scratch(batch_rows, d)
                        + _block_scratch(batch_rows, d) + _block_scratch(tq, d)
                        + [pltpu.SMEM((2,), _F32)]),
        compiler_params=pltpu.CompilerParams(
            dimension_semantics=("arbitrary", "arbitrary"), vmem_limit_bytes=VMEM_LIMIT_BYTES),
        name="attn_prompt",
    )(q, q, q, q, kt, vt, tt)


def _attention_sample(q, kb_new, vb_new, cache_kt, cache_vt, tq):
    bsz, d, past = cache_kt.shape
    tt = _cumsum_matrix()
    new_spec = pl.BlockSpec((tq, d), lambda b: (b, 0))
    cache_spec = pl.BlockSpec((None, d, past), lambda b: (b, 0, 0))
    return pl.pallas_call(
        _attn_sample_kernel,
        grid=(bsz,),
        in_specs=[new_spec, new_spec, new_spec, cache_spec, cache_spec,
                  pl.BlockSpec(tt.shape, lambda b: (0, 0))],
        out_specs=new_spec,
        out_shape=jax.ShapeDtypeStruct(q.shape, _BF16),
        scratch_shapes=_state_scratch(1, tq, d) + _block_scratch(tq, d),
        compiler_params=pltpu.CompilerParams(
            dimension_semantics=("arbitrary",), vmem_limit_bytes=VMEM_LIMIT_BYTES),
        name="attn_sample",
    )(q, kb_new, vb_new, cache_kt, cache_vt, tt)


def _ffn_kernel(x_ref, mc_ref, ga_ref, at_ref, wo_ref, gf_ref, wu_ref, wd_ref, gl_ref, y_ref,
                h_ref, hn_ref, *, f_chunk, final_norm):
    mixed = mc_ref[...] + ga_ref[...] * at_ref[...]
    h = x_ref[...] + jnp.dot(mixed, wo_ref[...], preferred_element_type=_F32)
    h_ref[...] = h
    hn_ref[...] = (h * _rms_scale(h) * gf_ref[...]).astype(_BF16)
    d_ff = wu_ref.shape[1]
    acc = h_ref[...]
    for c in range(d_ff // f_chunk):
        cs = slice(c * f_chunk, (c + 1) * f_chunk)
        up = jnp.dot(hn_ref[...], wu_ref[:, cs], preferred_element_type=_F32)
        act = jnp.square(jnp.maximum(up, 0.0)).astype(_BF16)
        acc = acc + jnp.dot(act, wd_ref[cs, :], preferred_element_type=_F32)
    y_ref[...] = acc * _rms_scale(acc) * gl_ref[...] if final_norm else acc


def _merge_ffn(x2, mc, ga, attn, wo_bf, g_ffn, wu_bf, wd_bf, g_final, *, tm, final_norm):
    rows, d = x2.shape
    d_ff = wu_bf.shape[1]
    row_spec = pl.BlockSpec((tm, d), lambda i: (i, 0))
    const = lambda i: (0, 0)
    once = pl.Buffered(1)
    return pl.pallas_call(
        functools.partial(_ffn_kernel, f_chunk=d, final_norm=final_norm),
        grid=(rows // tm,),
        in_specs=[row_spec, row_spec, row_spec, row_spec,
                  pl.BlockSpec((d, d), const, pipeline_mode=once),
                  pl.BlockSpec((1, d), const),
                  pl.BlockSpec((d, d_ff), const, pipeline_mode=once),
                  pl.BlockSpec((d_ff, d), const, pipeline_mode=once),
                  pl.BlockSpec((1, d), const)],
        out_specs=row_spec,
        out_shape=jax.ShapeDtypeStruct((rows, d), _F32),
        scratch_shapes=[pltpu.VMEM((tm, d), _F32), pltpu.VMEM((tm, d), _BF16)],
        compiler_params=pltpu.CompilerParams(
            dimension_semantics=("arbitrary",), vmem_limit_bytes=VMEM_LIMIT_BYTES),
        name="merge_ffn",
    )(x2, mc, ga, attn, wo_bf, g_ffn.reshape(1, d), wu_bf, wd_bf, g_final.reshape(1, d))


def _row_tile(rows, limit=512):
    tm = min(rows, limit)
    assert rows % tm == 0 and tm % CONV_TAIL == 0
    return tm


def kernel(x_prompt, x_sample, cache_conv, cache_k, cache_v, g_mix, w_in, conv_w, w_out,
           g_ffn, w_up, w_down, g_final):
    depth = w_in.shape[0]
    bsz, seq, d = x_prompt.shape
    dbsz, dseq, _ = x_sample.shape
    past, n_heads, head_dim = cache_k.shape[2], cache_k.shape[3], cache_k.shape[4]
    assert n_heads * head_dim == d and head_dim * HEADS_PER_SLAB == LANES
    assert seq % KEY_BLOCK == 0 and seq // KEY_BLOCK >= TILES_IN_FLIGHT and past % KEY_BLOCK == 0
    assert CONV_WIDTH - 1 <= dseq <= KEY_BLOCK

    def feature_major(c):
        return jnp.transpose(c, (0, 2, 3, 1)).reshape(c.shape[0], d, c.shape[1])

    def position_major(ct):
        return jnp.transpose(ct, (0, 3, 1, 2))

    xp, xs = x_prompt, x_sample
    outs = {name: [] for name in ("conv_p", "k_p", "v_p", "conv_s", "k_s", "v_s")}
    for l in range(depth):
        w_bf = w_in[l].astype(_BF16)
        wkv_t = w_bf[:, 4 * d:6 * d].T
        wo_bf = w_out[l].astype(_BF16)
        wu_bf = w_up[l].astype(_BF16)
        wd_bf = w_down[l].astype(_BF16)

        tm = _row_tile(seq)
        q, kt, vt, ktb, vtb, mc, ga, tail = _project(xp, g_mix[l], w_bf, conv_w[l], tm=tm, seg=tm,
                                                     n_heads=n_heads, wkv_t=wkv_t)
        attn = _attention_prompt(q, ktb, vtb)
        outs["conv_p"].append(tail[:, CONV_TAIL - (CONV_WIDTH - 1):, :])
        outs["k_p"].append(position_major(kt))
        outs["v_p"].append(position_major(vt))
        xp = _merge_ffn(xp.reshape(bsz * seq, d), mc, ga, attn, wo_bf, g_ffn[l], wu_bf, wd_bf,
                        g_final, tm=_row_tile(bsz * seq, FFN_ROWS), final_norm=l == depth - 1
                        ).reshape(bsz, seq, d)

        rows = dbsz * dseq
        hist = cache_conv[l].astype(_F32)
        prev1 = jnp.zeros((dbsz, dseq, d), _F32).at[:, 0].set(hist[:, 1]).reshape(rows, d)
        prev2 = (jnp.zeros((dbsz, dseq, d), _F32).at[:, 0].set(hist[:, 0]).at[:, 1].set(hist[:, 1])
                 .reshape(rows, d))
        q, k, v, kb, vb, mc, ga, u = _project(xs, g_mix[l], w_bf, conv_w[l], tm=rows, seg=dseq,
                                             n_heads=n_heads, prev=(prev1, prev2))
        attn = _attention_sample(q, kb, vb, feature_major(cache_k[l]), feature_major(cache_v[l]), dseq)
        outs["conv_s"].append(u.reshape(dbsz, dseq, d)[:, dseq - (CONV_WIDTH - 1):, :])
        outs["k_s"].append(k.reshape(dbsz, dseq, n_heads, head_dim))
        outs["v_s"].append(v.reshape(dbsz, dseq, n_heads, head_dim))
        xs = _merge_ffn(xs.reshape(rows, d), mc, ga, attn, wo_bf, g_ffn[l], wu_bf, wd_bf,
                        g_final, tm=_row_tile(rows), final_norm=l == depth - 1).reshape(dbsz, dseq, d)

    return (xp, xs, jnp.stack(outs["conv_p"]), jnp.stack(outs["k_p"]), jnp.stack(outs["v_p"]),
            jnp.stack(outs["conv_s"]), jnp.stack(outs["k_s"]), jnp.stack(outs["v_s"]))
```

```python
import functools

import jax
import jax.numpy as jnp
from jax import lax
from jax.experimental import pallas as pl
from jax.experimental.pallas import tpu as pltpu

RMS_EPS = 1e-6
CONV_WIDTH = 3
LANES = 128
KEY_BLOCK = 128
HEADS_PER_SLAB = 2
TILES_IN_FLIGHT = 3
FFN_ROWS = 1024
LATE_ROWS = 32
CONV_TAIL = 8
VMEM_LIMIT_BYTES = 60 * 1024 * 1024
CARRY_DONE = 104.0
LOG2_E = 1.4426950408889634
MASKED_SCORE = -1e30

_F32 = jnp.float32
_BF16 = jnp.bfloat16
_NT = (((1,), (1,)), ((), ()))


def _rms_scale(x32):
    return lax.rsqrt(jnp.mean(x32 * x32, axis=-1, keepdims=True) + RMS_EPS)


def _sigmoid(z):
    return 0.5 * jnp.tanh(0.5 * z) + 0.5


def _proj_kernel(*refs, d, tm, seg, scale):
    carried = seg == tm
    if carried:
        (x_ref, g_ref, w_ref, wkv_ref, cw_ref,
         q_ref, k_ref, v_ref, kb_ref, vb_ref, mc_ref, ga_ref, tail_ref,
         xn_ref, u_ref, conv_ref) = refs
    else:
        (x_ref, g_ref, w_ref, cw_ref, p1_ref, p2_ref,
         q_ref, k_ref, v_ref, kb_ref, vb_ref, mc_ref, ga_ref, tail_ref,
         xn_ref, u_ref, conv_ref) = refs
    b_g, c_g, h_g, q_g, k_g, v_g, gc_g, ga_g = range(8)

    if carried:
        @pl.when(pl.program_id(1) == 0)
        def _():
            u_ref[0:CONV_TAIL, :] = jnp.zeros((CONV_TAIL, d), _F32)

    x32 = x_ref[...]
    xn_ref[...] = (x32 * _rms_scale(x32) * g_ref[...]).astype(_BF16)

    def proj(group):
        return jnp.dot(xn_ref[...], w_ref[:, group * d:(group + 1) * d], preferred_element_type=_F32)

    u = proj(c_g) * proj(h_g)
    cw = cw_ref[...]
    if carried:
        u_ref[CONV_TAIL:CONV_TAIL + tm, :] = u
        with_history = u_ref[...]
        s1 = pltpu.roll(with_history, 1, 0)[CONV_TAIL:, :]
        s2 = pltpu.roll(with_history, 2, 0)[CONV_TAIL:, :]
        conv_ref[...] = cw[0:1, :] * s2 + cw[1:2, :] * s1 + cw[2:3, :] * u
        tail = u[tm - CONV_TAIL:tm, :]
        u_ref[0:CONV_TAIL, :] = tail
        tail_ref[...] = tail
    else:
        row = lax.broadcasted_iota(jnp.int32, (tm, d), 0) % seg
        s1 = jnp.where(row == 0, p1_ref[...], pltpu.roll(u, 1, 0))
        s2 = jnp.where(row < 2, p2_ref[...], pltpu.roll(u, 2, 0))
        conv_ref[...] = cw[0:1, :] * s2 + cw[1:2, :] * s1 + cw[2:3, :] * u
        tail_ref[...] = u

    conv_out = proj(b_g) * conv_ref[...]
    mc_ref[...] = (_sigmoid(proj(gc_g)) * conv_out).astype(_BF16)
    ga_ref[...] = _sigmoid(proj(ga_g)).astype(_BF16)
    q_ref[...] = (proj(q_g) * scale).astype(_BF16)
    if carried:
        for i, (o_ref, ob_ref) in enumerate(((k_ref, kb_ref), (v_ref, vb_ref))):
            t = lax.dot_general(wkv_ref[i * d:(i + 1) * d, :], xn_ref[...], _NT,
                                preferred_element_type=_F32)
            o_ref[...] = t.reshape(o_ref.shape)
            ob_ref[...] = t.astype(_BF16)
    else:
        for group, o_ref, ob_ref in ((k_g, k_ref, kb_ref), (v_g, v_ref, vb_ref)):
            t = proj(group)
            o_ref[...] = t.reshape(o_ref.shape)
            ob_ref[...] = t.astype(_BF16)


def _project(x, g, w_bf, conv_w, *, tm, seg, n_heads, wkv_t=None, prev=None):
    bsz, t, d = x.shape
    carried = prev is None
    n_t = t // tm if carried else 1
    rows = bsz * t
    x2 = x.reshape(rows, d)
    head_dim = d // n_heads
    scale = float(head_dim) ** -0.5

    if carried:
        grid = (bsz, n_t)
        row_map = lambda b, j: (b * n_t + j, 0)
    else:
        assert rows == tm
        grid = (1, 1)
        row_map = lambda b, j: (0, 0)
    const = lambda b, j: (0, 0)
    once = pl.Buffered(1)
    row_spec = pl.BlockSpec((tm, d), row_map)
    in_specs = [row_spec, pl.BlockSpec((1, d), const), pl.BlockSpec(w_bf.shape, const, pipeline_mode=once)]
    args = [x2, g.reshape(1, d), w_bf]
    if carried:
        in_specs.append(pl.BlockSpec(wkv_t.shape, const, pipeline_mode=once))
        args.append(wkv_t)
    in_specs.append(pl.BlockSpec((CONV_WIDTH, d), const))
    args.append(conv_w)
    bf = jax.ShapeDtypeStruct((rows, d), _BF16)
    if carried:
        kv = jax.ShapeDtypeStruct((bsz, n_heads, head_dim, t), _F32)
        kv_spec = pl.BlockSpec((None, n_heads, head_dim, tm), lambda b, j: (b, 0, 0, j))
        kvb = jax.ShapeDtypeStruct((bsz, d, t), _BF16)
        kvb_spec = pl.BlockSpec((None, d, tm), lambda b, j: (b, 0, j))
        tail_shape = jax.ShapeDtypeStruct((bsz, CONV_TAIL, d), _F32)
        tail_spec = pl.BlockSpec((None, CONV_TAIL, d), lambda b, j: (b, 0, 0))
    else:
        in_specs += [row_spec, row_spec]
        args += [prev[0], prev[1]]
        kv = jax.ShapeDtypeStruct((rows, n_heads, head_dim), _F32)
        kv_spec = pl.BlockSpec((tm, n_heads, head_dim), lambda b, j: (0, 0, 0))
        kvb, kvb_spec = bf, row_spec
        tail_shape = jax.ShapeDtypeStruct((rows, d), _F32)
        tail_spec = row_spec
    out_shape = [bf, kv, kv, kvb, kvb, bf, bf, tail_shape]
    out_specs = [row_spec, kv_spec, kv_spec, kvb_spec, kvb_spec, row_spec, row_spec, tail_spec]
    return pl.pallas_call(
        functools.partial(_proj_kernel, d=d, tm=tm, seg=seg, scale=scale),
        grid=grid, in_specs=in_specs, out_specs=out_specs, out_shape=out_shape,
        scratch_shapes=[pltpu.VMEM((tm, d), _BF16),
                        pltpu.VMEM((tm + CONV_TAIL, d), _F32),
                        pltpu.VMEM((tm, d), _F32)],
        compiler_params=pltpu.CompilerParams(
            dimension_semantics=("arbitrary", "arbitrary"), vmem_limit_bytes=VMEM_LIMIT_BYTES),
        name="proj_carried" if carried else "proj_segmented",
    )(*args)


def _stages(tiles, d, keys_on_lanes):
    n = len(tiles)
    starts = [sum(t[4] for t in tiles[:r]) for r in range(n + 1)]
    m = starts[n]
    n_slabs = d // LANES
    half = LANES // HEADS_PER_SLAB
    rows_per_slab = HEADS_PER_SLAB * m

    def _rows(x, r):
        return x[starts[r]:starts[r + 1]]

    def block_diagonal(blk, p):
        if keys_on_lanes:
            x = blk[p * LANES:(p + 1) * LANES, :]
            zero = jnp.zeros((half, KEY_BLOCK), x.dtype)
            return jnp.concatenate([jnp.concatenate([x[:half], zero], axis=1),
                                    jnp.concatenate([zero, x[half:]], axis=1)], axis=0)
        x = blk[:, p * LANES:(p + 1) * LANES]
        first_head = lax.broadcasted_iota(jnp.int32, x.shape, 1) < half
        return jnp.concatenate([jnp.where(first_head, x, jnp.zeros_like(x)),
                                jnp.where(first_head, jnp.zeros_like(x), x)], axis=0)

    def masked(s):
        pieces = [_rows(s, r) if t[2] is None else jnp.where(t[2], _rows(s, r), MASKED_SCORE)
                  for r, t in enumerate(tiles)]
        return jnp.concatenate(pieces, axis=0) if n > 1 else pieces[0]

    def scores_and_softplus(p, kblk, s_ref, hl_ref):
        sl = slice(p * LANES, (p + 1) * LANES)
        kbd = block_diagonal(kblk, p)
        q_rows = [t[0][t[3]:t[3] + t[4], sl] for t in tiles]
        q2 = jnp.concatenate(q_rows, axis=0) if n > 1 else q_rows[0]
        if keys_on_lanes:
            s = jnp.dot(q2, kbd, preferred_element_type=_F32)
        else:
            s = lax.dot_general(q2, kbd, _NT, preferred_element_type=_F32)
        s = masked(s)
        s_ref[p, 0:m, :] = s
        soft = jnp.maximum(s, 0.0) + jnp.log(1.0 + jnp.exp2(jnp.abs(s) * -LOG2_E))
        hi = soft.astype(_BF16)
        lo = (soft - hi.astype(_F32)).astype(_BF16)
        for h in range(HEADS_PER_SLAB):
            hs = slice(h * KEY_BLOCK, (h + 1) * KEY_BLOCK)
            r0 = (HEADS_PER_SLAB * p + h) * m
            hl_ref[r0:r0 + m, :] = jnp.concatenate([hi[:, hs], lo[:, hs]], axis=1)

    def weights_and_values(p, vblk, cr, carry_ref, oacc_ref, s_ref):
        sl = slice(p * LANES, (p + 1) * LANES)
        vbd = block_diagonal(vblk, p)
        per_head = [cr[h * m:(h + 1) * m] for h in range(HEADS_PER_SLAB)]
        csum = jnp.concatenate([c[:, :KEY_BLOCK] for c in per_head], axis=1)
        rsum = jnp.concatenate([c[:, KEY_BLOCK:] for c in per_head], axis=1)
        below = s_ref[p, 0:m, :] - csum
        args, carries = [], []
        for r, (_, slot, _, row0, rows, fresh) in enumerate(tiles):
            if fresh:
                args.append(_rows(below, r))
                carries.append(_rows(rsum, r))
            else:
                carry = carry_ref[slot, p, row0:row0 + rows]
                args.append(_rows(below, r) - carry)
                carries.append(carry + _rows(rsum, r))
        a = jnp.exp(jnp.concatenate(args, axis=0) if n > 1 else args[0]).astype(_BF16)
        if keys_on_lanes:
            o = lax.dot_general(a, vbd, _NT, preferred_element_type=_F32)
        else:
            o = jnp.dot(a, vbd, preferred_element_type=_F32)
        for r, (_, slot, _, row0, rows, fresh) in enumerate(tiles):
            if fresh:
                oacc_ref[slot, row0:row0 + rows, sl] = _rows(o, r)
            else:
                oacc_ref[slot, row0:row0 + rows, sl] += _rows(o, r)
            carry_ref[slot, p, row0:row0 + rows] = carries[r]
        late = carries[n - 2][LATE_ROWS:] if n > 1 else None
        return carries[n - 1], late

    def scores(kblk, s_ref, hl_ref):
        for p in range(n_slabs):
            scores_and_softplus(p, kblk, s_ref, hl_ref)

    def cumsum(hl_ref, tt_ref):
        cr = jnp.dot(hl_ref[0:n_slabs * rows_per_slab, :], tt_ref[...], preferred_element_type=_F32)
        return lambda p: cr[p * rows_per_slab:(p + 1) * rows_per_slab]

    def values(vblk, cr_of_slab, carry_ref, oacc_ref, s_ref, before_slab=None):
        smallest = smallest_late = None
        for p in range(n_slabs):
            if before_slab is not None:
                before_slab(p)
            last, late = weights_and_values(p, vblk, cr_of_slab(p), carry_ref, oacc_ref, s_ref)
            smallest = last if smallest is None else jnp.minimum(smallest, last)
            if late is not None:
                smallest_late = late if smallest_late is None else jnp.minimum(smallest_late, late)
        return jnp.min(smallest), None if smallest_late is None else jnp.min(smallest_late)

    return scores, cumsum, values, scores_and_softplus


def _attend(tiles, kblk, vblk, keys_on_lanes, tt_ref, carry_ref, oacc_ref, s_ref, hl_ref):
    scores, cumsum, values, _ = _stages(tiles, kblk.shape[0 if keys_on_lanes else 1], keys_on_lanes)
    scores(kblk, s_ref, hl_ref)
    return values(vblk, cumsum(hl_ref, tt_ref), carry_ref, oacc_ref, s_ref)


def _causal_mask(tq):
    row = lax.broadcasted_iota(jnp.int32, (tq, KEY_BLOCK), 0)
    col = lax.broadcasted_iota(jnp.int32, (tq, KEY_BLOCK), 1)
    m = col < row
    return jnp.concatenate([m] * HEADS_PER_SLAB, axis=1)


def _walk_back(first_block, smallest, load_block, tile, tt_ref, carry_ref, oacc_ref, s_ref, hl_ref):
    def cond(state):
        kb, m = state
        return jnp.logical_and(kb >= 0, m <= CARRY_DONE)

    def body(state):
        kb, _ = state
        kblk, vblk = load_block(kb)
        m, _ = _attend([tile], kblk, vblk, True, tt_ref, carry_ref, oacc_ref, s_ref, hl_ref)
        return kb - 1, m

    lax.while_loop(cond, body, (jnp.asarray(first_block, jnp.int32), smallest))


def _attn_prompt_kernel(q0_ref, q1_ref, q2_ref, q3_ref, k_ref, v_ref, tt_ref, o_ref,
                        carry_ref, oacc_ref, sa_ref, hla_ref, sb_ref, hlb_ref, s1_ref, hl1_ref, m_ref,
                        *, n_blocks):
    g = pl.program_id(1)
    ja = n_blocks - 1 - g
    jb = ja + 1
    tq, d = q0_ref.shape
    slots = [lax.rem(jb + TILES_IN_FLIGHT + r, TILES_IN_FLIGHT) for r in range(TILES_IN_FLIGHT)]
    generic = (tt_ref, carry_ref, oacc_ref, s1_ref, hl1_ref)

    def load_block(kb):
        start = pl.multiple_of(jnp.clip(kb, 0, n_blocks - 1) * KEY_BLOCK, KEY_BLOCK)
        return k_ref[:, pl.ds(start, KEY_BLOCK)], v_ref[:, pl.ds(start, KEY_BLOCK)]

    @pl.when(g == 0)
    def _():
        carry_ref[...] = jnp.zeros(carry_ref.shape, _F32)
        oacc_ref[...] = jnp.zeros(oacc_ref.shape, _F32)
        m_ref[0] = 0.0
        m_ref[1] = 0.0

    late_rows_done = m_ref[1] > CARRY_DONE
    last_tile_live = jb + TILES_IN_FLIGHT - 1 < n_blocks
    buffers = ((sa_ref, hla_ref), (sb_ref, hlb_ref))

    def step(new, old):
        mask = _causal_mask(tq)
        rows = ((0, tq, True), (0, tq, False), (0, LATE_ROWS, False))
        q_refs = (q0_ref, q1_ref, q2_ref)
        tiles_a = [(q_refs[r], None, mask if r == 0 else None) + rows[r] for r in range(TILES_IN_FLIGHT)]
        tiles_b = [(None, slots[r], mask if r == 0 else None) + rows[r] for r in range(TILES_IN_FLIGHT)]
        scores_a, _, _, scores_slab_a = _stages(tiles_a, d, True)
        _, cumsum_b, values_b, _ = _stages(tiles_b, d, True)
        kblk = load_block(ja)[0]
        if old is None:
            scores_a(kblk, *new)
            return
        between = None if new is None else (lambda p: scores_slab_a(p, kblk, *new))
        m_ref[0], m_ref[1] = values_b(load_block(jb)[1], cumsum_b(old[1], tt_ref), carry_ref, oacc_ref,
                                      old[0], before_slab=between)

    pl.when(g == 0)(lambda: step(buffers[0], None))
    pl.when(g == n_blocks)(lambda: step(None, buffers[(n_blocks - 1) % 2]))

    @pl.when(jnp.logical_and(g > 0, g < n_blocks))
    def _():
        even = lax.rem(g, 2) == 0
        pl.when(even)(lambda: step(buffers[0], buffers[1]))
        pl.when(jnp.logical_not(even))(lambda: step(buffers[1], buffers[0]))

    @pl.when(jnp.logical_and(jnp.logical_and(last_tile_live, jb >= 0), jnp.logical_not(late_rows_done)))
    def _():
        late_tile = (q3_ref, slots[2], None, LATE_ROWS, tq - LATE_ROWS, False)
        m_late, _ = _attend([late_tile], *load_block(jb), True, *generic)
        m_ref[0] = jnp.minimum(m_ref[0], m_late)

    first_block = jnp.where(last_tile_live, jb - 1, -1)
    _walk_back(first_block, m_ref[0], load_block, (q3_ref, slots[2], None, 0, tq, False), *generic)
    o_ref[...] = oacc_ref[slots[2]].astype(_BF16)


def _attn_sample_kernel(q_ref, kn_ref, vn_ref, kc_ref, vc_ref, tt_ref, o_ref,
                        carry_ref, oacc_ref, s_ref, hl_ref):
    tq, d = q_ref.shape
    past = kc_ref.shape[1]
    pad = jnp.zeros((KEY_BLOCK - tq, d), _BF16)
    kblk = jnp.concatenate([kn_ref[...], pad], axis=0)
    vblk = jnp.concatenate([vn_ref[...], pad], axis=0)
    m, _ = _attend([(q_ref, 0, _causal_mask(tq), 0, tq, True)], kblk, vblk, False,
                   tt_ref, carry_ref, oacc_ref, s_ref, hl_ref)

    def load_block(kb):
        start = pl.multiple_of(kb * KEY_BLOCK, KEY_BLOCK)
        return (kc_ref[:, pl.ds(start, KEY_BLOCK)].astype(_BF16),
                vc_ref[:, pl.ds(start, KEY_BLOCK)].astype(_BF16))

    _walk_back(past // KEY_BLOCK - 1, m, load_block, (q_ref, 0, None, 0, tq, False),
               tt_ref, carry_ref, oacc_ref, s_ref, hl_ref)
    o_ref[...] = oacc_ref[0].astype(_BF16)


def _cumsum_matrix():
    j = lax.broadcasted_iota(jnp.int32, (KEY_BLOCK, KEY_BLOCK), 0)
    s = lax.broadcasted_iota(jnp.int32, (KEY_BLOCK, KEY_BLOCK), 1)
    half = jnp.concatenate([(j >= s).astype(_BF16), jnp.ones((KEY_BLOCK, KEY_BLOCK), _BF16)], axis=1)
    return jnp.concatenate([half, half], axis=0)


def _state_scratch(n_tiles, tq, d):
    width = HEADS_PER_SLAB * KEY_BLOCK
    return [pltpu.VMEM((n_tiles, d // LANES, tq, width), _F32),
            pltpu.VMEM((n_tiles, tq, d), _F32)]


def _block_scratch(rows, d):
    n_slabs = d // LANES
    width = HEADS_PER_SLAB * KEY_BLOCK
    return [pltpu.VMEM((n_slabs, rows, width), _F32),
            pltpu.VMEM((HEADS_PER_SLAB * n_slabs * rows, width), _BF16)]


def _attention_prompt(q, kt, vt):
    bsz, d, seq = kt.shape
    tq = KEY_BLOCK
    n_blocks = seq // tq
    tt = _cumsum_matrix()
    seq_spec = pl.BlockSpec((None, d, seq), lambda b, g: (b, 0, 0))

    def q_spec(r):
        return pl.BlockSpec((tq, d), lambda b, g: (b * n_blocks + jnp.clip(n_blocks - 1 - g + r, 0, n_blocks - 1), 0))

    batch_rows = (TILES_IN_FLIGHT - 1) * tq + LATE_ROWS
    return pl.pallas_call(
        functools.partial(_attn_prompt_kernel, n_blocks=n_blocks),
        grid=(bsz, n_blocks + TILES_IN_FLIGHT),
        in_specs=[q_spec(0), q_spec(1), q_spec(2), q_spec(3), seq_spec, seq_spec,
                  pl.BlockSpec(tt.shape, lambda b, g: (0, 0))],
        out_specs=q_spec(TILES_IN_FLIGHT),
        out_shape=jax.ShapeDtypeStruct(q.shape, _BF16),
        scratch_shapes=(_state_scratch(TILES_IN_FLIGHT, tq, d) + _block_scratch(batch_rows, d)
                        + _block_scratch(batch_rows, d) + _block_scratch(tq, d)
                        + [pltpu.SMEM((2,), _F32)]),
        compiler_params=pltpu.CompilerParams(
            dimension_semantics=("arbitrary", "arbitrary"), vmem_limit_bytes=VMEM_LIMIT_BYTES),
        name="attn_prompt",
    )(q, q, q, q, kt, vt, tt)


def _attention_sample(q, kb_new, vb_new, cache_kt, cache_vt, tq):
    bsz, d, past = cache_kt.shape
    tt = _cumsum_matrix()
    new_spec = pl.BlockSpec((tq, d), lambda b: (b, 0))
    cache_spec = pl.BlockSpec((None, d, past), lambda b: (b, 0, 0))
    return pl.pallas_call(
        _attn_sample_kernel,
        grid=(bsz,),
        in_specs=[new_spec, new_spec, new_spec, cache_spec, cache_spec,
                  pl.BlockSpec(tt.shape, lambda b: (0, 0))],
        out_specs=new_spec,
        out_shape=jax.ShapeDtypeStruct(q.shape, _BF16),
        scratch_shapes=_state_scratch(1, tq, d) + _block_scratch(tq, d),
        compiler_params=pltpu.CompilerParams(
            dimension_semantics=("arbitrary",), vmem_limit_bytes=VMEM_LIMIT_BYTES),
        name="attn_sample",
    )(q, kb_new, vb_new, cache_kt, cache_vt, tt)


def _ffn_kernel(x_ref, mc_ref, ga_ref, at_ref, wo_ref, gf_ref, wu_ref, wd_ref, gl_ref, y_ref,
                h_ref, hn_ref, *, f_chunk, final_norm):
    mixed = mc_ref[...] + ga_ref[...] * at_ref[...]
    h = x_ref[...] + jnp.dot(mixed, wo_ref[...], preferred_element_type=_F32)
    h_ref[...] = h
    hn_ref[...] = (h * _rms_scale(h) * gf_ref[...]).astype(_BF16)
    d_ff = wu_ref.shape[1]
    acc = h_ref[...]
    for c in range(d_ff // f_chunk):
        cs = slice(c * f_chunk, (c + 1) * f_chunk)
        up = jnp.dot(hn_ref[...], wu_ref[:, cs], preferred_element_type=_F32)
        act = jnp.square(jnp.maximum(up, 0.0)).astype(_BF16)
        acc = acc + jnp.dot(act, wd_ref[cs, :], preferred_element_type=_F32)
    y_ref[...] = acc * _rms_scale(acc) * gl_ref[...] if final_norm else acc


def _merge_ffn(x2, mc, ga, attn, wo_bf, g_ffn, wu_bf, wd_bf, g_final, *, tm, final_norm):
    rows, d = x2.shape
    d_ff = wu_bf.shape[1]
    row_spec = pl.BlockSpec((tm, d), lambda i: (i, 0))
    const = lambda i: (0, 0)
    once = pl.Buffered(1)
    return pl.pallas_call(
        functools.partial(_ffn_kernel, f_chunk=d, final_norm=final_norm),
        grid=(rows // tm,),
        in_specs=[row_spec, row_spec, row_spec, row_spec,
                  pl.BlockSpec((d, d), const, pipeline_mode=once),
                  pl.BlockSpec((1, d), const),
                  pl.BlockSpec((d, d_ff), const, pipeline_mode=once),
                  pl.BlockSpec((d_ff, d), const, pipeline_mode=once),
                  pl.BlockSpec((1, d), const)],
        out_specs=row_spec,
        out_shape=jax.ShapeDtypeStruct((rows, d), _F32),
        scratch_shapes=[pltpu.VMEM((tm, d), _F32), pltpu.VMEM((tm, d), _BF16)],
        compiler_params=pltpu.CompilerParams(
            dimension_semantics=("arbitrary",), vmem_limit_bytes=VMEM_LIMIT_BYTES),
        name="merge_ffn",
    )(x2, mc, ga, attn, wo_bf, g_ffn.reshape(1, d), wu_bf, wd_bf, g_final.reshape(1, d))


def _row_tile(rows, limit=512):
    tm = min(rows, limit)
    assert rows % tm == 0 and tm % CONV_TAIL == 0
    return tm


def kernel(x_prompt, x_sample, cache_conv, cache_k, cache_v, g_mix, w_in, conv_w, w_out,
           g_ffn, w_up, w_down, g_final):
    depth = w_in.shape[0]
    bsz, seq, d = x_prompt.shape
    dbsz, dseq, _ = x_sample.shape
    past, n_heads, head_dim = cache_k.shape[2], cache_k.shape[3], cache_k.shape[4]
    assert n_heads * head_dim == d and head_dim * HEADS_PER_SLAB == LANES
    assert seq % KEY_BLOCK == 0 and seq // KEY_BLOCK >= TILES_IN_FLIGHT and past % KEY_BLOCK == 0
    assert CONV_WIDTH - 1 <= dseq <= KEY_BLOCK

    def feature_major(c):
        return jnp.transpose(c, (0, 2, 3, 1)).reshape(c.shape[0], d, c.shape[1])

    def position_major(ct):
        return jnp.transpose(ct, (0, 3, 1, 2))

    xp, xs = x_prompt, x_sample
    outs = {name: [] for name in ("conv_p", "k_p", "v_p", "conv_s", "k_s", "v_s")}
    for l in range(depth):
        w_bf = w_in[l].astype(_BF16)
        wkv_t = w_bf[:, 4 * d:6 * d].T
        wo_bf = w_out[l].astype(_BF16)
        wu_bf = w_up[l].astype(_BF16)
        wd_bf = w_down[l].astype(_BF16)

        tm = _row_tile(seq)
        q, kt, vt, ktb, vtb, mc, ga, tail = _project(xp, g_mix[l], w_bf, conv_w[l], tm=tm, seg=tm,
                                                     n_heads=n_heads, wkv_t=wkv_t)
        attn = _attention_prompt(q, ktb, vtb)
        outs["conv_p"].append(tail[:, CONV_TAIL - (CONV_WIDTH - 1):, :])
        outs["k_p"].append(position_major(kt))
        outs["v_p"].append(position_major(vt))
        xp = _merge_ffn(xp.reshape(bsz * seq, d), mc, ga, attn, wo_bf, g_ffn[l], wu_bf, wd_bf,
                        g_final, tm=_row_tile(bsz * seq, FFN_ROWS), final_norm=l == depth - 1
                        ).reshape(bsz, seq, d)

        rows = dbsz * dseq
        hist = cache_conv[l].astype(_F32)
        prev1 = jnp.zeros((dbsz, dseq, d), _F32).at[:, 0].set(hist[:, 1]).reshape(rows, d)
        prev2 = (jnp.zeros((dbsz, dseq, d), _F32).at[:, 0].set(hist[:, 0]).at[:, 1].set(hist[:, 1])
                 .reshape(rows, d))
        q, k, v, kb, vb, mc, ga, u = _project(xs, g_mix[l], w_bf, conv_w[l], tm=rows, seg=dseq,
                                             n_heads=n_heads, prev=(prev1, prev2))
        attn = _attention_sample(q, kb, vb, feature_major(cache_k[l]), feature_major(cache_v[l]), dseq)
        outs["conv_s"].append(u.reshape(dbsz, dseq, d)[:, dseq - (CONV_WIDTH - 1):, :])
        outs["k_s"].append(k.reshape(dbsz, dseq, n_heads, head_dim))
        outs["v_s"].append(v.reshape(dbsz, dseq, n_heads, head_dim))
        xs = _merge_ffn(xs.reshape(rows, d), mc, ga, attn, wo_bf, g_ffn[l], wu_bf, wd_bf,
                        g_final, tm=_row_tile(rows), final_norm=l == depth - 1).reshape(dbsz, dseq, d)

    return (xp, xs, jnp.stack(outs["conv_p"]), jnp.stack(outs["k_p"]), jnp.stack(outs["v_p"]),
            jnp.stack(outs["conv_s"]), jnp.stack(outs["k_s"]), jnp.stack(outs["v_s"]))
```

```python
import functools

import jax
import jax.numpy as jnp
from jax import lax
from jax.experimental import pallas as pl
from jax.experimental.pallas import tpu as pltpu

RMS_EPS = 1e-6
CONV_WIDTH = 3
LANES = 128
KEY_BLOCK = 128
HEADS_PER_SLAB = 2
TILES_IN_FLIGHT = 3
FFN_ROWS = 1024
LATE_ROWS = 32
CONV_TAIL = 8
VMEM_LIMIT_BYTES = 60 * 1024 * 1024
CARRY_DONE = 104.0
LOG2_E = 1.4426950408889634
MASKED_SCORE = -1e30

_F32 = jnp.float32
_BF16 = jnp.bfloat16
_NT = (((1,), (1,)), ((), ()))


def _rms_scale(x32):
    return lax.rsqrt(jnp.mean(x32 * x32, axis=-1, keepdims=True) + RMS_EPS)


def _sigmoid(z):
    return 0.5 * jnp.tanh(0.5 * z) + 0.5


def _proj_kernel(*refs, d, tm, seg, scale):
    carried = seg == tm
    if carried:
        (x_ref, g_ref, w_ref, wkv_ref, cw_ref, xnext_ref,
         q_ref, k_ref, v_ref, kb_ref, vb_ref, mc_ref, ga_ref, tail_ref,
         xn_ref, u_ref, conv_ref) = refs
    else:
        (x_ref, g_ref, w_ref, cw_ref, p1_ref, p2_ref,
         q_ref, k_ref, v_ref, kb_ref, vb_ref, mc_ref, ga_ref, tail_ref,
         xn_ref, u_ref, conv_ref) = refs
    b_g, c_g, h_g, q_g, k_g, v_g, gc_g, ga_g = range(8)

    if carried:
        @pl.when(pl.program_id(1) == 0)
        def _():
            u_ref[0:CONV_TAIL, :] = jnp.zeros((CONV_TAIL, d), _F32)

    def normed(ref):
        x32 = ref[...]
        return (x32 * _rms_scale(x32) * g_ref[...]).astype(_BF16)

    if carried:
        tile = pl.program_id(0) * pl.num_programs(1) + pl.program_id(1)
        half = lax.rem(tile, 2)

        @pl.when(tile == 0)
        def _():
            xn_ref[0] = normed(x_ref)

        xn_cur = xn_ref.at[half]
    else:
        xn_cur = xn_ref.at[0]
        xn_cur[...] = normed(x_ref)

    def proj(group):
        return jnp.dot(xn_cur[...], w_ref[:, group * d:(group + 1) * d], preferred_element_type=_F32)

    u = proj(c_g) * proj(h_g)
    cw = cw_ref[...]
    if carried:
        u_ref[CONV_TAIL:CONV_TAIL + tm, :] = u
        with_history = u_ref[...]
        s1 = pltpu.roll(with_history, 1, 0)[CONV_TAIL:, :]
        s2 = pltpu.roll(with_history, 2, 0)[CONV_TAIL:, :]
        conv_ref[...] = cw[0:1, :] * s2 + cw[1:2, :] * s1 + cw[2:3, :] * u
        tail = u[tm - CONV_TAIL:tm, :]
        u_ref[0:CONV_TAIL, :] = tail
        tail_ref[...] = tail
    else:
        row = lax.broadcasted_iota(jnp.int32, (tm, d), 0) % seg
        s1 = jnp.where(row == 0, p1_ref[...], pltpu.roll(u, 1, 0))
        s2 = jnp.where(row < 2, p2_ref[...], pltpu.roll(u, 2, 0))
        conv_ref[...] = cw[0:1, :] * s2 + cw[1:2, :] * s1 + cw[2:3, :] * u
        tail_ref[...] = u

    conv_out = proj(b_g) * conv_ref[...]
    mc_ref[...] = (_sigmoid(proj(gc_g)) * conv_out).astype(_BF16)
    ga_ref[...] = _sigmoid(proj(ga_g)).astype(_BF16)
    q_ref[...] = (proj(q_g) * scale).astype(_BF16)
    if carried:
        for i, (o_ref, ob_ref) in enumerate(((k_ref, kb_ref), (v_ref, vb_ref))):
            t = lax.dot_general(wkv_ref[i * d:(i + 1) * d, :], xn_cur[...], _NT,
                                preferred_element_type=_F32)
            o_ref[...] = t.reshape(o_ref.shape)
            ob_ref[...] = t.astype(_BF16)
        xn_ref[1 - half] = normed(xnext_ref)
    else:
        for group, o_ref, ob_ref in ((k_g, k_ref, kb_ref), (v_g, v_ref, vb_ref)):
            t = proj(group)
            o_ref[...] = t.reshape(o_ref.shape)
            ob_ref[...] = t.astype(_BF16)


def _project(x, g, w_bf, conv_w, *, tm, seg, n_heads, wkv_t=None, prev=None):
    bsz, t, d = x.shape
    carried = prev is None
    n_t = t // tm if carried else 1
    rows = bsz * t
    x2 = x.reshape(rows, d)
    head_dim = d // n_heads
    scale = float(head_dim) ** -0.5

    if carried:
        grid = (bsz, n_t)
        row_map = lambda b, j: (b * n_t + j, 0)
    else:
        assert rows == tm
        grid = (1, 1)
        row_map = lambda b, j: (0, 0)
    const = lambda b, j: (0, 0)
    once = pl.Buffered(1)
    row_spec = pl.BlockSpec((tm, d), row_map)
    in_specs = [row_spec, pl.BlockSpec((1, d), const), pl.BlockSpec(w_bf.shape, const, pipeline_mode=once)]
    args = [x2, g.reshape(1, d), w_bf]
    if carried:
        in_specs.append(pl.BlockSpec(wkv_t.shape, const, pipeline_mode=once))
        args.append(wkv_t)
    in_specs.append(pl.BlockSpec((CONV_WIDTH, d), const))
    args.append(conv_w)
    if carried:
        last = bsz * n_t - 1
        in_specs.append(pl.BlockSpec((tm, d), lambda b, j: (jnp.minimum(b * n_t + j + 1, last), 0)))
        args.append(x2)
    bf = jax.ShapeDtypeStruct((rows, d), _BF16)
    if carried:
        kv = jax.ShapeDtypeStruct((bsz, n_heads, head_dim, t), _F32)
        kv_spec = pl.BlockSpec((None, n_heads, head_dim, tm), lambda b, j: (b, 0, 0, j))
        kvb = jax.ShapeDtypeStruct((bsz, d, t), _BF16)
        kvb_spec = pl.BlockSpec((None, d, tm), lambda b, j: (b, 0, j))
        tail_shape = jax.ShapeDtypeStruct((bsz, CONV_TAIL, d), _F32)
        tail_spec = pl.BlockSpec((None, CONV_TAIL, d), lambda b, j: (b, 0, 0))
    else:
        in_specs += [row_spec, row_spec]
        args += [prev[0], prev[1]]
        kv = jax.ShapeDtypeStruct((rows, n_heads, head_dim), _F32)
        kv_spec = pl.BlockSpec((tm, n_heads, head_dim), lambda b, j: (0, 0, 0))
        kvb, kvb_spec = bf, row_spec
        tail_shape = jax.ShapeDtypeStruct((rows, d), _F32)
        tail_spec = row_spec
    out_shape = [bf, kv, kv, kvb, kvb, bf, bf, tail_shape]
    out_specs = [row_spec, kv_spec, kv_spec, kvb_spec, kvb_spec, row_spec, row_spec, tail_spec]
    return pl.pallas_call(
        functools.partial(_proj_kernel, d=d, tm=tm, seg=seg, scale=scale),
        grid=grid, in_specs=in_specs, out_specs=out_specs, out_shape=out_shape,
        scratch_shapes=[pltpu.VMEM((2, tm, d), _BF16),
                        pltpu.VMEM((tm + CONV_TAIL, d), _F32),
                        pltpu.VMEM((tm, d), _F32)],
        compiler_params=pltpu.CompilerParams(
            dimension_semantics=("arbitrary", "arbitrary"), vmem_limit_bytes=VMEM_LIMIT_BYTES),
        name="proj_carried" if carried else "proj_segmented",
    )(*args)


def _stages(tiles, d, keys_on_lanes):
    n = len(tiles)
    starts = [sum(t[4] for t in tiles[:r]) for r in range(n + 1)]
    m = starts[n]
    n_slabs = d // LANES
    half = LANES // HEADS_PER_SLAB
    rows_per_slab = HEADS_PER_SLAB * m

    def _rows(x, r):
        return x[starts[r]:starts[r + 1]]

    def block_diagonal(blk, p):
        if keys_on_lanes:
            x = blk[p * LANES:(p + 1) * LANES, :]
            zero = jnp.zeros((half, KEY_BLOCK), x.dtype)
            return jnp.concatenate([jnp.concatenate([x[:half], zero], axis=1),
                                    jnp.concatenate([zero, x[half:]], axis=1)], axis=0)
        x = blk[:, p * LANES:(p + 1) * LANES]
        first_head = lax.broadcasted_iota(jnp.int32, x.shape, 1) < half
        return jnp.concatenate([jnp.where(first_head, x, jnp.zeros_like(x)),
                                jnp.where(first_head, jnp.zeros_like(x), x)], axis=0)

    def masked(s):
        pieces = [_rows(s, r) if t[2] is None else jnp.where(t[2], _rows(s, r), MASKED_SCORE)
                  for r, t in enumerate(tiles)]
        return jnp.concatenate(pieces, axis=0) if n > 1 else pieces[0]

    def scores_and_softplus(p, kblk, s_ref, hl_ref):
        sl = slice(p * LANES, (p + 1) * LANES)
        kbd = block_diagonal(kblk, p)
        q_rows = [t[0][t[3]:t[3] + t[4], sl] for t in tiles]
        q2 = jnp.concatenate(q_rows, axis=0) if n > 1 else q_rows[0]
        if keys_on_lanes:
            s = jnp.dot(q2, kbd, preferred_element_type=_F32)
        else:
            s = lax.dot_general(q2, kbd, _NT, preferred_element_type=_F32)
        s = masked(s)
        s_ref[p, 0:m, :] = s
        soft = jnp.maximum(s, 0.0) + jnp.log(1.0 + jnp.exp2(jnp.abs(s) * -LOG2_E))
        hi = soft.astype(_BF16)
        lo = (soft - hi.astype(_F32)).astype(_BF16)
        for h in range(HEADS_PER_SLAB):
            hs = slice(h * KEY_BLOCK, (h + 1) * KEY_BLOCK)
            r0 = (HEADS_PER_SLAB * p + h) * m
            hl_ref[r0:r0 + m, :] = jnp.concatenate([hi[:, hs], lo[:, hs]], axis=1)

    def weights_and_values(p, vblk, cr, carry_ref, oacc_ref, s_ref):
        sl = slice(p * LANES, (p + 1) * LANES)
        vbd = block_diagonal(vblk, p)
        per_head = [cr[h * m:(h + 1) * m] for h in range(HEADS_PER_SLAB)]
        csum = jnp.concatenate([c[:, :KEY_BLOCK] for c in per_head], axis=1)
        rsum = jnp.concatenate([c[:, KEY_BLOCK:] for c in per_head], axis=1)
        below = s_ref[p, 0:m, :] - csum
        args, carries = [], []
        for r, (_, slot, _, row0, rows, fresh) in enumerate(tiles):
            if fresh:
                args.append(_rows(below, r))
                carries.append(_rows(rsum, r))
            else:
                carry = carry_ref[slot, p, row0:row0 + rows]
                args.append(_rows(below, r) - carry)
                carries.append(carry + _rows(rsum, r))
        a = jnp.exp(jnp.concatenate(args, axis=0) if n > 1 else args[0]).astype(_BF16)
        if keys_on_lanes:
            o = lax.dot_general(a, vbd, _NT, preferred_element_type=_F32)
        else:
            o = jnp.dot(a, vbd, preferred_element_type=_F32)
        for r, (_, slot, _, row0, rows, fresh) in enumerate(tiles):
            if fresh:
                oacc_ref[slot, row0:row0 + rows, sl] = _rows(o, r)
            else:
                oacc_ref[slot, row0:row0 + rows, sl] += _rows(o, r)
            carry_ref[slot, p, row0:row0 + rows] = carries[r]
        late = carries[n - 2][LATE_ROWS:] if n > 1 else None
        return carries[n - 1], late

    def scores(kblk, s_ref, hl_ref):
        for p in range(n_slabs):
            scores_and_softplus(p, kblk, s_ref, hl_ref)

    def cumsum(hl_ref, tt_ref):
        cr = jnp.dot(hl_ref[0:n_slabs * rows_per_slab, :], tt_ref[...], preferred_element_type=_F32)
        return lambda p: cr[p * rows_per_slab:(p + 1) * rows_per_slab]

    def values(vblk, cr_of_slab, carry_ref, oacc_ref, s_ref, before_slab=None):
        smallest = smallest_late = None
        for p in range(n_slabs):
            if before_slab is not None:
                before_slab(p)
            last, late = weights_and_values(p, vblk, cr_of_slab(p), carry_ref, oacc_ref, s_ref)
            smallest = last if smallest is None else jnp.minimum(smallest, last)
            if late is not None:
                smallest_late = late if smallest_late is None else jnp.minimum(smallest_late, late)
        return jnp.min(smallest), None if smallest_late is None else jnp.min(smallest_late)

    return scores, cumsum, values, scores_and_softplus


def _attend(tiles, kblk, vblk, keys_on_lanes, tt_ref, carry_ref, oacc_ref, s_ref, hl_ref):
    scores, cumsum, values, _ = _stages(tiles, kblk.shape[0 if keys_on_lanes else 1], keys_on_lanes)
    scores(kblk, s_ref, hl_ref)
    return values(vblk, cumsum(hl_ref, tt_ref), carry_ref, oacc_ref, s_ref)


def _causal_mask(tq):
    row = lax.broadcasted_iota(jnp.int32, (tq, KEY_BLOCK), 0)
    col = lax.broadcasted_iota(jnp.int32, (tq, KEY_BLOCK), 1)
    m = col < row
    return jnp.concatenate([m] * HEADS_PER_SLAB, axis=1)


def _walk_back(first_block, smallest, load_block, tile, tt_ref, carry_ref, oacc_ref, s_ref, hl_ref):
    def cond(state):
        kb, m = state
        return jnp.logical_and(kb >= 0, m <= CARRY_DONE)

    def body(state):
        kb, _ = state
        kblk, vblk = load_block(kb)
        m, _ = _attend([tile], kblk, vblk, True, tt_ref, carry_ref, oacc_ref, s_ref, hl_ref)
        return kb - 1, m

    lax.while_loop(cond, body, (jnp.asarray(first_block, jnp.int32), smallest))


def _attn_prompt_kernel(q0_ref, q1_ref, q2_ref, q3_ref, k_ref, v_ref, tt_ref, o_ref,
                        carry_ref, oacc_ref, sa_ref, hla_ref, sb_ref, hlb_ref, s1_ref, hl1_ref, m_ref,
                        *, n_blocks):
    g = pl.program_id(1)
    ja = n_blocks - 1 - g
    jb = ja + 1
    tq, d = q0_ref.shape
    slots = [lax.rem(jb + TILES_IN_FLIGHT + r, TILES_IN_FLIGHT) for r in range(TILES_IN_FLIGHT)]
    generic = (tt_ref, carry_ref, oacc_ref, s1_ref, hl1_ref)

    def load_block(kb):
        start = pl.multiple_of(jnp.clip(kb, 0, n_blocks - 1) * KEY_BLOCK, KEY_BLOCK)
        return k_ref[:, pl.ds(start, KEY_BLOCK)], v_ref[:, pl.ds(start, KEY_BLOCK)]

    @pl.when(g == 0)
    def _():
        carry_ref[...] = jnp.zeros(carry_ref.shape, _F32)
        oacc_ref[...] = jnp.zeros(oacc_ref.shape, _F32)
        m_ref[0] = 0.0
        m_ref[1] = 0.0

    late_rows_done = m_ref[1] > CARRY_DONE
    last_tile_live = jb + TILES_IN_FLIGHT - 1 < n_blocks
    buffers = ((sa_ref, hla_ref), (sb_ref, hlb_ref))

    def step(new, old):
        mask = _causal_mask(tq)
        rows = ((0, tq, True), (0, tq, False), (0, LATE_ROWS, False))
        q_refs = (q0_ref, q1_ref, q2_ref)
        tiles_a = [(q_refs[r], None, mask if r == 0 else None) + rows[r] for r in range(TILES_IN_FLIGHT)]
        tiles_b = [(None, slots[r], mask if r == 0 else None) + rows[r] for r in range(TILES_IN_FLIGHT)]
        scores_a, _, _, scores_slab_a = _stages(tiles_a, d, True)
        _, cumsum_b, values_b, _ = _stages(tiles_b, d, True)
        kblk = load_block(ja)[0]
        if old is None:
            scores_a(kblk, *new)
            return
        between = None if new is None else (lambda p: scores_slab_a(p, kblk, *new))
        m_ref[0], m_ref[1] = values_b(load_block(jb)[1], cumsum_b(old[1], tt_ref), carry_ref, oacc_ref,
                                      old[0], before_slab=between)

    pl.when(g == 0)(lambda: step(buffers[0], None))
    pl.when(g == n_blocks)(lambda: step(None, buffers[(n_blocks - 1) % 2]))

    @pl.when(jnp.logical_and(g > 0, g < n_blocks))
    def _():
        even = lax.rem(g, 2) == 0
        pl.when(even)(lambda: step(buffers[0], buffers[1]))
        pl.when(jnp.logical_not(even))(lambda: step(buffers[1], buffers[0]))

    @pl.when(jnp.logical_and(jnp.logical_and(last_tile_live, jb >= 0), jnp.logical_not(late_rows_done)))
    def _():
        late_tile = (q3_ref, slots[2], None, LATE_ROWS, tq - LATE_ROWS, False)
        m_late, _ = _attend([late_tile], *load_block(jb), True, *generic)
        m_ref[0] = jnp.minimum(m_ref[0], m_late)

    first_block = jnp.where(last_tile_live, jb - 1, -1)
    _walk_back(first_block, m_ref[0], load_block, (q3_ref, slots[2], None, 0, tq, False), *generic)
    o_ref[...] = oacc_ref[slots[2]].astype(_BF16)


def _attn_sample_kernel(q_ref, kn_ref, vn_ref, kc_ref, vc_ref, tt_ref, o_ref,
                        carry_ref, oacc_ref, s_ref, hl_ref):
    tq, d = q_ref.shape
    past = kc_ref.shape[1]
    pad = jnp.zeros((KEY_BLOCK - tq, d), _BF16)
    kblk = jnp.concatenate([kn_ref[...], pad], axis=0)
    vblk = jnp.concatenate([vn_ref[...], pad], axis=0)
    m, _ = _attend([(q_ref, 0, _causal_mask(tq), 0, tq, True)], kblk, vblk, False,
                   tt_ref, carry_ref, oacc_ref, s_ref, hl_ref)

    def load_block(kb):
        start = pl.multiple_of(kb * KEY_BLOCK, KEY_BLOCK)
        return (kc_ref[:, pl.ds(start, KEY_BLOCK)].astype(_BF16),
                vc_ref[:, pl.ds(start, KEY_BLOCK)].astype(_BF16))

    _walk_back(past // KEY_BLOCK - 1, m, load_block, (q_ref, 0, None, 0, tq, False),
               tt_ref, carry_ref, oacc_ref, s_ref, hl_ref)
    o_ref[...] = oacc_ref[0].astype(_BF16)


def _cumsum_matrix():
    j = lax.broadcasted_iota(jnp.int32, (KEY_BLOCK, KEY_BLOCK), 0)
    s = lax.broadcasted_iota(jnp.int32, (KEY_BLOCK, KEY_BLOCK), 1)
    half = jnp.concatenate([(j >= s).astype(_BF16), jnp.ones((KEY_BLOCK, KEY_BLOCK), _BF16)], axis=1)
    return jnp.concatenate([half, half], axis=0)


def _state_scratch(n_tiles, tq, d):
    width = HEADS_PER_SLAB * KEY_BLOCK
    return [pltpu.VMEM((n_tiles, d // LANES, tq, width), _F32),
            pltpu.VMEM((n_tiles, tq, d), _F32)]


def _block_scratch(rows, d):
    n_slabs = d // LANES
    width = HEADS_PER_SLAB * KEY_BLOCK
    return [pltpu.VMEM((n_slabs, rows, width), _F32),
            pltpu.VMEM((HEADS_PER_SLAB * n_slabs * rows, width), _BF16)]


def _attention_prompt(q, kt, vt):
    bsz, d, seq = kt.shape
    tq = KEY_BLOCK
    n_blocks = seq // tq
    tt = _cumsum_matrix()
    seq_spec = pl.BlockSpec((None, d, seq), lambda b, g: (b, 0, 0))

    def q_spec(r):
        return pl.BlockSpec((tq, d), lambda b, g: (b * n_blocks + jnp.clip(n_blocks - 1 - g + r, 0, n_blocks - 1), 0))

    batch_rows = (TILES_IN_FLIGHT - 1) * tq + LATE_ROWS
    return pl.pallas_call(
        functools.partial(_attn_prompt_kernel, n_blocks=n_blocks),
        grid=(bsz, n_blocks + TILES_IN_FLIGHT),
        in_specs=[q_spec(0), q_spec(1), q_spec(2), q_spec(3), seq_spec, seq_spec,
                  pl.BlockSpec(tt.shape, lambda b, g: (0, 0))],
        out_specs=q_spec(TILES_IN_FLIGHT),
        out_shape=jax.ShapeDtypeStruct(q.shape, _BF16),
        scratch_shapes=(_state_scratch(TILES_IN_FLIGHT, tq, d) + _block_scratch(batch_rows, d)
                        + _block_scratch(batch_rows, d) + _block_scratch(tq, d)
                        + [pltpu.SMEM((2,), _F32)]),
        compiler_params=pltpu.CompilerParams(
            dimension_semantics=("arbitrary", "arbitrary"), vmem_limit_bytes=VMEM_LIMIT_BYTES),
        name="attn_prompt",
    )(q, q, q, q, kt, vt, tt)


def _attention_sample(q, kb_new, vb_new, cache_kt, cache_vt, tq):
    bsz, d, past = cache_kt.shape
    tt = _cumsum_matrix()
    new_spec = pl.BlockSpec((tq, d), lambda b: (b, 0))
    cache_spec = pl.BlockSpec((None, d, past), lambda b: (b, 0, 0))
    return pl.pallas_call(
        _attn_sample_kernel,
        grid=(bsz,),
        in_specs=[new_spec, new_spec, new_spec, cache_spec, cache_spec,
                  pl.BlockSpec(tt.shape, lambda b: (0, 0))],
        out_specs=new_spec,
        out_shape=jax.ShapeDtypeStruct(q.shape, _BF16),
        scratch_shapes=_state_scratch(1, tq, d) + _block_scratch(tq, d),
        compiler_params=pltpu.CompilerParams(
            dimension_semantics=("arbitrary",), vmem_limit_bytes=VMEM_LIMIT_BYTES),
        name="attn_sample",
    )(q, kb_new, vb_new, cache_kt, cache_vt, tt)


def _ffn_kernel(x_ref, mc_ref, ga_ref, at_ref, wo_ref, gf_ref, wu_ref, wd_ref, gl_ref, y_ref,
                h_ref, hn_ref, *, f_chunk, final_norm):
    mixed = mc_ref[...] + ga_ref[...] * at_ref[...]
    h = x_ref[...] + jnp.dot(mixed, wo_ref[...], preferred_element_type=_F32)
    h_ref[...] = h
    hn_ref[...] = (h * _rms_scale(h) * gf_ref[...]).astype(_BF16)
    d_ff = wu_ref.shape[1]
    acc = h_ref[...]
    for c in range(d_ff // f_chunk):
        cs = slice(c * f_chunk, (c + 1) * f_chunk)
        up = jnp.dot(hn_ref[...], wu_ref[:, cs], preferred_element_type=_F32)
        act = jnp.square(jnp.maximum(up, 0.0)).astype(_BF16)
        acc = acc + jnp.dot(act, wd_ref[cs, :], preferred_element_type=_F32)
    y_ref[...] = acc * _rms_scale(acc) * gl_ref[...] if final_norm else acc


def _merge_ffn(x2, mc, ga, attn, wo_bf, g_ffn, wu_bf, wd_bf, g_final, *, tm, final_norm):
    rows, d = x2.shape
    d_ff = wu_bf.shape[1]
    row_spec = pl.BlockSpec((tm, d), lambda i: (i, 0))
    const = lambda i: (0, 0)
    once = pl.Buffered(1)
    return pl.pallas_call(
        functools.partial(_ffn_kernel, f_chunk=d, final_norm=final_norm),
        grid=(rows // tm,),
        in_specs=[row_spec, row_spec, row_spec, row_spec,
                  pl.BlockSpec((d, d), const, pipeline_mode=once),
                  pl.BlockSpec((1, d), const),
                  pl.BlockSpec((d, d_ff), const, pipeline_mode=once),
                  pl.BlockSpec((d_ff, d), const, pipeline_mode=once),
                  pl.BlockSpec((1, d), const)],
        out_specs=row_spec,
        out_shape=jax.ShapeDtypeStruct((rows, d), _F32),
        scratch_shapes=[pltpu.VMEM((tm, d), _F32), pltpu.VMEM((tm, d), _BF16)],
        compiler_params=pltpu.CompilerParams(
            dimension_semantics=("arbitrary",), vmem_limit_bytes=VMEM_LIMIT_BYTES),
        name="merge_ffn",
    )(x2, mc, ga, attn, wo_bf, g_ffn.reshape(1, d), wu_bf, wd_bf, g_final.reshape(1, d))


def _row_tile(rows, limit=512):
    tm = min(rows, limit)
    assert rows % tm == 0 and tm % CONV_TAIL == 0
    return tm


def kernel(x_prompt, x_sample, cache_conv, cache_k, cache_v, g_mix, w_in, conv_w, w_out,
           g_ffn, w_up, w_down, g_final):
    depth = w_in.shape[0]
    bsz, seq, d = x_prompt.shape
    dbsz, dseq, _ = x_sample.shape
    past, n_heads, head_dim = cache_k.shape[2], cache_k.shape[3], cache_k.shape[4]
    assert n_heads * head_dim == d and head_dim * HEADS_PER_SLAB == LANES
    assert seq % KEY_BLOCK == 0 and seq // KEY_BLOCK >= TILES_IN_FLIGHT and past % KEY_BLOCK == 0
    assert CONV_WIDTH - 1 <= dseq <= KEY_BLOCK

    def feature_major(c):
        return jnp.transpose(c, (0, 2, 3, 1)).reshape(c.shape[0], d, c.shape[1])

    def position_major(ct):
        return jnp.transpose(ct, (0, 3, 1, 2))

    xp, xs = x_prompt, x_sample
    outs = {name: [] for name in ("conv_p", "k_p", "v_p", "conv_s", "k_s", "v_s")}
    for l in range(depth):
        w_bf = w_in[l].astype(_BF16)
        wkv_t = w_bf[:, 4 * d:6 * d].T
        wo_bf = w_out[l].astype(_BF16)
        wu_bf = w_up[l].astype(_BF16)
        wd_bf = w_down[l].astype(_BF16)

        tm = _row_tile(seq)
        q, kt, vt, ktb, vtb, mc, ga, tail = _project(xp, g_mix[l], w_bf, conv_w[l], tm=tm, seg=tm,
                                                     n_heads=n_heads, wkv_t=wkv_t)
        attn = _attention_prompt(q, ktb, vtb)
        outs["conv_p"].append(tail[:, CONV_TAIL - (CONV_WIDTH - 1):, :])
        outs["k_p"].append(position_major(kt))
        outs["v_p"].append(position_major(vt))
        xp = _merge_ffn(xp.reshape(bsz * seq, d), mc, ga, attn, wo_bf, g_ffn[l], wu_bf, wd_bf,
                        g_final, tm=_row_tile(bsz * seq, FFN_ROWS), final_norm=l == depth - 1
                        ).reshape(bsz, seq, d)

        rows = dbsz * dseq
        hist = cache_conv[l].astype(_F32)
        prev1 = jnp.zeros((dbsz, dseq, d), _F32).at[:, 0].set(hist[:, 1]).reshape(rows, d)
        prev2 = (jnp.zeros((dbsz, dseq, d), _F32).at[:, 0].set(hist[:, 0]).at[:, 1].set(hist[:, 1])
                 .reshape(rows, d))
        q, k, v, kb, vb, mc, ga, u = _project(xs, g_mix[l], w_bf, conv_w[l], tm=rows, seg=dseq,
                                             n_heads=n_heads, prev=(prev1, prev2))
        attn = _attention_sample(q, kb, vb, feature_major(cache_k[l]), feature_major(cache_v[l]), dseq)
        outs["conv_s"].append(u.reshape(dbsz, dseq, d)[:, dseq - (CONV_WIDTH - 1):, :])
        outs["k_s"].append(k.reshape(dbsz, dseq, n_heads, head_dim))
        outs["v_s"].append(v.reshape(dbsz, dseq, n_heads, head_dim))
        xs = _merge_ffn(xs.reshape(rows, d), mc, ga, attn, wo_bf, g_ffn[l], wu_bf, wd_bf,
                        g_final, tm=_row_tile(rows), final_norm=l == depth - 1).reshape(dbsz, dseq, d)

    return (xp, xs, jnp.stack(outs["conv_p"]), jnp.stack(outs["k_p"]), jnp.stack(outs["v_p"]),
            jnp.stack(outs["conv_s"]), jnp.stack(outs["k_s"]), jnp.stack(outs["v_s"]))
```
